```python
import math
import jax, jax.numpy as jnp
from jax import lax
import numpy as np

D_MODEL = 2048
BATCH = 2
SEQ = 16384
DEPTH = 4
DEC_BATCH = 8
DEC_SEQ = 32
PAST_LEN = 2048

CHUNK = 64
N_HEADS = 8
HEAD_DIM = D_MODEL // N_HEADS // 2
D_FF = 5632
CONV_WIDTH = 3
Q_BLOCK = 128
N_ATTN = (DEPTH + 1) // 2
N_CONV = DEPTH // 2
EPS = 1e-5
SCALE = HEAD_DIM ** -0.5

kernel_name = "chunk_streaming_diffattn_shortconv_macaron"


def _rms(x, g):
    xf = x.astype(jnp.float32)
    y = xf * lax.rsqrt(jnp.mean(xf * xf, axis=-1, keepdims=True) + EPS)
    return (y * g.astype(jnp.float32)).astype(x.dtype)


def _swiglu(h, w_gu, w_down):
    g, u = jnp.split(h @ w_gu, 2, axis=-1)
    return (jax.nn.silu(g) * u) @ w_down


def _diff_lambda(lq, lk, lam_init):
    lq = lq.astype(jnp.float32)
    lk = lk.astype(jnp.float32)
    return jnp.exp(jnp.sum(lq[0] * lk[0])) - jnp.exp(jnp.sum(lq[1] * lk[1])) + lam_init


def _qkv(h, w_qkv):
    b, t, _ = h.shape
    q, k, v = jnp.split(h @ w_qkv, 3, axis=-1)
    q = q.reshape(b, t, 2 * N_HEADS, HEAD_DIM)
    k = k.reshape(b, t, 2 * N_HEADS, HEAD_DIM)
    v = v.reshape(b, t, N_HEADS, 2 * HEAD_DIM)
    return q, k, v


def _diff_core(q, k, v, lam, subln_g, lam_init, mask):
    s = jnp.einsum('bqhd,bkhd->bhqk', q.astype(jnp.float32), k.astype(jnp.float32)) * SCALE
    if mask is not None:
        s = jnp.where(mask[None, None], s, -jnp.inf)
    p = jax.nn.softmax(s, axis=-1)
    b, _, nq, nk = p.shape
    p = p.reshape(b, N_HEADS, 2, nq, nk)
    a = p[:, :, 0] - lam * p[:, :, 1]
    o = jnp.einsum('bhqk,bkhe->bqhe', a, v.astype(jnp.float32))
    o = _rms(o, subln_g) * (1.0 - lam_init)
    return o.astype(v.dtype)


def _attn_prompt(h, w_qkv, w_o, lam, subln_g, lam_init):
    b, s, _ = h.shape
    q, k, v = _qkv(h, w_qkv)
    nqb = s // Q_BLOCK
    qb = jnp.moveaxis(q.reshape(b, nqb, Q_BLOCK, 2 * N_HEADS, HEAD_DIM), 1, 0)
    k_chunk = jnp.arange(s) // CHUNK

    def block(args):
        q_i, b_i = args
        q_chunk = (b_i * Q_BLOCK + jnp.arange(Q_BLOCK)) // CHUNK
        mask = k_chunk[None, :] <= q_chunk[:, None]
        return _diff_core(q_i, k, v, lam, subln_g, lam_init, mask)

    o = lax.map(block, (qb, jnp.arange(nqb)))
    o = jnp.moveaxis(o, 0, 1).reshape(b, s, D_MODEL)
    return o @ w_o, k, v


def _attn_sample(h, cache_k, cache_v, w_qkv, w_o, lam, subln_g, lam_init):
    b, t, _ = h.shape
    q, k, v = _qkv(h, w_qkv)
    k_all = jnp.concatenate([cache_k.astype(k.dtype), k], axis=1)
    v_all = jnp.concatenate([cache_v.astype(v.dtype), v], axis=1)
    o = _diff_core(q, k_all, v_all, lam, subln_g, lam_init, None)
    return o.reshape(b, t, D_MODEL) @ w_o, k, v


def _short_conv(h, prev, w_in, w_conv, w_out):
    b_gate, c_gate, xt = jnp.split(h @ w_in, 3, axis=-1)
    g = c_gate * xt
    if prev is None:
        prev = jnp.zeros((h.shape[0], CONV_WIDTH - 1, D_MODEL), g.dtype)
    padded = jnp.concatenate([prev.astype(g.dtype), g], axis=1)
    y = lax.conv_general_dilated(
        padded, w_conv[:, None, :].astype(g.dtype), window_strides=(1,), padding='VALID',
        dimension_numbers=('NWC', 'WIO', 'NWC'), feature_group_count=D_MODEL)
    return (b_gate * y) @ w_out, padded[:, -(CONV_WIDTH - 1):]


def setup_inputs(seed: int = 0) -> dict:
    key = jax.random.key(seed)
    ks = jax.random.split(key, 20)
    f32 = jnp.float32
    nrm = lambda k, shape, scale: jax.random.normal(k, shape, f32) * scale
    return {
        "x_prompt": nrm(ks[0], (BATCH, SEQ, D_MODEL), 1.0),
        "x_sample": nrm(ks[1], (DEC_BATCH, DEC_SEQ, D_MODEL), 1.0),
        "cache_k_l0": nrm(ks[2], (DEC_BATCH, PAST_LEN, 2 * N_HEADS, HEAD_DIM), 1.0),
        "cache_v_l0": nrm(ks[3], (DEC_BATCH, PAST_LEN, N_HEADS, 2 * HEAD_DIM), 1.0),
        "state_conv_l1": nrm(ks[4], (DEC_BATCH, CONV_WIDTH - 1, D_MODEL), 1.0),
        "cache_k_l2": nrm(ks[5], (DEC_BATCH, PAST_LEN, 2 * N_HEADS, HEAD_DIM), 1.0),
        "cache_v_l2": nrm(ks[6], (DEC_BATCH, PAST_LEN, N_HEADS, 2 * HEAD_DIM), 1.0),
        "state_conv_l3": nrm(ks[7], (DEC_BATCH, CONV_WIDTH - 1, D_MODEL), 1.0),
        "norm_g": 1.0 + nrm(ks[8], (DEPTH, 3, D_MODEL), 0.01),
        "final_norm_g": 1.0 + nrm(ks[9], (D_MODEL,), 0.01),
        "ffn_w_gu": nrm(ks[10], (DEPTH, 2, D_MODEL, 2 * D_FF), D_MODEL ** -0.5),
        "ffn_w_down": nrm(ks[11], (DEPTH, 2, D_FF, D_MODEL), D_FF ** -0.5),
        "attn_w_qkv": nrm(ks[12], (N_ATTN, D_MODEL, 3 * D_MODEL), D_MODEL ** -0.5),
        "attn_w_o": nrm(ks[13], (N_ATTN, D_MODEL, D_MODEL), D_MODEL ** -0.5),
        "attn_lambda_q": nrm(ks[14], (N_ATTN, 2, HEAD_DIM), 0.1),
        "attn_lambda_k": nrm(ks[15], (N_ATTN, 2, HEAD_DIM), 0.1),
        "attn_subln_g": 1.0 + nrm(ks[16], (N_ATTN, 2 * HEAD_DIM), 0.01),
        "conv_w_in": nrm(ks[17], (N_CONV, D_MODEL, 3 * D_MODEL), D_MODEL ** -0.5),
        "conv_w": nrm(ks[18], (N_CONV, CONV_WIDTH, D_MODEL), CONV_WIDTH ** -0.5),
        "conv_w_out": nrm(ks[19], (N_CONV, D_MODEL, D_MODEL), D_MODEL ** -0.5),
    }


def reference(x_prompt, x_sample, cache_k_l0, cache_v_l0, state_conv_l1, cache_k_l2, cache_v_l2,
              state_conv_l3, norm_g, final_norm_g, ffn_w_gu, ffn_w_down, attn_w_qkv, attn_w_o,
              attn_lambda_q, attn_lambda_k, attn_subln_g, conv_w_in, conv_w, conv_w_out):
    attn_caches = ((cache_k_l0, cache_v_l0), (cache_k_l2, cache_v_l2))
    conv_states = (state_conv_l1, state_conv_l3)
    xp, xs = x_prompt, x_sample
    new_p, new_s = [], []
    for i in range(DEPTH):
        xp = xp + 0.5 * _swiglu(_rms(xp, norm_g[i, 0]), ffn_w_gu[i, 0], ffn_w_down[i, 0])
        xs = xs + 0.5 * _swiglu(_rms(xs, norm_g[i, 0]), ffn_w_gu[i, 0], ffn_w_down[i, 0])
        hp = _rms(xp, norm_g[i, 1])
        hs = _rms(xs, norm_g[i, 1])
        j = i // 2
        if i % 2 == 0:
            lam_init = 0.8 - 0.6 * math.exp(-0.3 * i)
            lam = _diff_lambda(attn_lambda_q[j], attn_lambda_k[j], lam_init)
            mp, kp, vp = _attn_prompt(hp, attn_w_qkv[j], attn_w_o[j], lam, attn_subln_g[j], lam_init)
            ck, cv = attn_caches[j]
            ms, k_s, v_s = _attn_sample(hs, ck, cv, attn_w_qkv[j], attn_w_o[j], lam,
                                        attn_subln_g[j], lam_init)
            new_p += [kp, vp]
            new_s += [k_s, v_s]
        else:
            mp, sp = _short_conv(hp, None, conv_w_in[j], conv_w[j], conv_w_out[j])
            ms, ss = _short_conv(hs, conv_states[j], conv_w_in[j], conv_w[j], conv_w_out[j])
            new_p += [sp]
            new_s += [ss]
        xp = xp + mp
        xs = xs + ms
        xp = xp + 0.5 * _swiglu(_rms(xp, norm_g[i, 2]), ffn_w_gu[i, 1], ffn_w_down[i, 1])
        xs = xs + 0.5 * _swiglu(_rms(xs, norm_g[i, 2]), ffn_w_gu[i, 1], ffn_w_down[i, 1])
    y_prompt = _rms(xp, final_norm_g)
    y_sample = _rms(xs, final_norm_g)
    k_l0_p, v_l0_p, c_l1_p, k_l2_p, v_l2_p, c_l3_p = new_p
    k_l0_s, v_l0_s, c_l1_s, k_l2_s, v_l2_s, c_l3_s = new_s
    return (y_prompt, y_sample, k_l0_p, v_l0_p, c_l1_p, k_l2_p, v_l2_p, c_l3_p,
            k_l0_s, v_l0_s, c_l1_s, k_l2_s, v_l2_s, c_l3_s)
```

```python
import functools
import math

import jax
import jax.numpy as jnp
from jax import lax
from jax.experimental import pallas as pl
from jax.experimental.pallas import tpu as pltpu

F32 = jnp.float32
BF16 = jnp.bfloat16

EPS = 1e-5
N_HEADS = 8
HEAD_DIM = 128
HEAD_WIDTH = 2 * HEAD_DIM
CHUNK = 64
CONV_WIDTH = 3
CARRY_ROWS = 8

V7X_VMEM_BYTES = 64 * 1024 * 1024
VMEM_LIMIT = V7X_VMEM_BYTES - 8 * 1024 * 1024


def _params(*semantics):
    return pltpu.CompilerParams(dimension_semantics=semantics, vmem_limit_bytes=VMEM_LIMIT)


def _rms_rows(x, g):
    ms = jnp.mean(x * x, axis=-1, keepdims=True)
    return x * lax.rsqrt(ms + EPS) * g


def _ffn_kernel(x_ref, g_ref, wg_ref, wu_ref, wd_ref, o_ref, h_ref):
    f = pl.program_id(1)

    @pl.when(f == 0)
    def _():
        x = x_ref[...]
        h_ref[...] = _rms_rows(x, g_ref[...]).astype(BF16)
        o_ref[...] = x

    h = h_ref[...]
    gate = jnp.dot(h, wg_ref[...], preferred_element_type=F32)
    up = jnp.dot(h, wu_ref[...], preferred_element_type=F32)
    a = (gate * jax.nn.sigmoid(gate) * up * 0.5).astype(BF16)
    o_ref[...] += jnp.dot(a, wd_ref[...], preferred_element_type=F32)


def _ffn(x, g, w_gu, w_down, layer, half, *, tm, tf):
    t, d = x.shape
    d_ff = w_down.shape[2]
    nf = d_ff // tf
    assert t % tm == 0 and d_ff % tf == 0
    return pl.pallas_call(
        _ffn_kernel,
        out_shape=jax.ShapeDtypeStruct((t, d), F32),
        grid=(t // tm, nf),
        in_specs=[
            pl.BlockSpec((tm, d), lambda i, f: (i, 0)),
            pl.BlockSpec((1, d), lambda i, f: (0, 0)),
            pl.BlockSpec((None, None, d, tf), lambda i, f: (layer, half, 0, f)),
            pl.BlockSpec((None, None, d, tf), lambda i, f: (layer, half, 0, f + nf)),
            pl.BlockSpec((None, None, tf, d), lambda i, f: (layer, half, f, 0)),
        ],
        out_specs=pl.BlockSpec((tm, d), lambda i, f: (i, 0)),
        scratch_shapes=[pltpu.VMEM((tm, d), BF16)],
        compiler_params=_params("parallel", "arbitrary"),
        name="ffn",
    )(x, g, w_gu, w_gu, w_down)


def _norm_proj_kernel(x_ref, g_ref, w_ref, *o_refs):
    h = _rms_rows(x_ref[...], g_ref[...]).astype(BF16)
    y = jnp.dot(h, w_ref[...], preferred_element_type=F32)
    for o_ref in o_refs:
        o_ref[...] = y.astype(o_ref.dtype)


def _norm_proj(x, g, w, layer, col, out_dtypes, *, tm):
    t, d = x.shape
    assert t % tm == 0
    outs = pl.pallas_call(
        _norm_proj_kernel,
        out_shape=[jax.ShapeDtypeStruct((t, d), dt) for dt in out_dtypes],
        grid=(t // tm,),
        in_specs=[
            pl.BlockSpec((tm, d), lambda i: (i, 0)),
            pl.BlockSpec((1, d), lambda i: (0, 0)),
            pl.BlockSpec((None, d, d), lambda i: (layer, 0, col)),
        ],
        out_specs=[pl.BlockSpec((tm, d), lambda i: (i, 0)) for _ in out_dtypes],
        compiler_params=_params("parallel"),
        name="norm_proj",
    )(x, g, w)
    return outs


def _proj_residual_kernel(x_ref, a_ref, w_ref, o_ref):
    o_ref[...] = x_ref[...] + jnp.dot(a_ref[...], w_ref[...], preferred_element_type=F32)


def _proj_residual(x, a, w, layer, *, tm):
    t, d = x.shape
    assert t % tm == 0
    return pl.pallas_call(
        _proj_residual_kernel,
        out_shape=jax.ShapeDtypeStruct((t, d), F32),
        grid=(t // tm,),
        in_specs=[
            pl.BlockSpec((tm, d), lambda i: (i, 0)),
            pl.BlockSpec((tm, d), lambda i: (i, 0)),
            pl.BlockSpec((None, d, d), lambda i: (layer, 0, 0)),
        ],
        out_specs=pl.BlockSpec((tm, d), lambda i: (i, 0)),
        compiler_params=_params("parallel"),
        name="proj_residual",
    )(x, a, w)


def _diff_lambda(lq_ref, lk_ref, lam_init):
    e = jnp.exp(jnp.sum(lq_ref[...] * lk_ref[...], axis=1, keepdims=True))
    return e[0:1] - e[1:2] + lam_init


def _sub_norm(o, sg, lam_init):
    return _rms_rows(o, sg) * (1.0 - lam_init)


def _scores(q, k):
    s = lax.dot_general(q, k, (((1,), (1,)), ((), ())), preferred_element_type=F32)
    return s * (HEAD_DIM ** -0.5)


def _attn_prompt_kernel(lq_ref, lk_ref, sg_ref, q_ref, k_ref, v_ref, o_ref, m_ref, l_ref, acc_ref,
                        *, lam_init, tq, tk):
    qi = pl.program_id(2)
    ki = pl.program_id(3)
    last_ki = ((qi + 1) * tq - 1) // tk

    @pl.when(ki == 0)
    def _():
        m_ref[...] = jnp.full(m_ref.shape, -jnp.inf, F32)
        l_ref[...] = jnp.zeros(l_ref.shape, F32)
        acc_ref[...] = jnp.zeros(acc_ref.shape, F32)

    def step(masked):
        v = v_ref[...]
        if masked:
            q_chunk = (qi * tq + lax.broadcasted_iota(jnp.int32, (tq, tk), 0)) // CHUNK
            k_chunk = (ki * tk + lax.broadcasted_iota(jnp.int32, (tq, tk), 1)) // CHUNK
            visible = k_chunk <= q_chunk
        for c in range(2):
            cols = slice(c * HEAD_DIM, (c + 1) * HEAD_DIM)
            s = _scores(q_ref[:, cols], k_ref[:, cols])
            if masked:
                s = jnp.where(visible, s, -jnp.inf)
            m_prev = m_ref[c]
            m_new = jnp.maximum(m_prev, jnp.max(s, axis=1, keepdims=True))
            alpha = jnp.exp(m_prev - m_new)
            p = jnp.exp(s - m_new)
            l_ref[c] = alpha * l_ref[c] + jnp.sum(p, axis=1, keepdims=True)
            acc_ref[c] = alpha * acc_ref[c] + jnp.dot(p.astype(BF16), v, preferred_element_type=F32)
            m_ref[c] = m_new

    fully_visible = (ki + 1) * tk <= (qi * tq // CHUNK + 1) * CHUNK

    @pl.when(jnp.logical_and(ki <= last_ki, fully_visible))
    def _():
        step(False)

    @pl.when(jnp.logical_and(ki <= last_ki, jnp.logical_not(fully_visible)))
    def _():
        step(True)

    @pl.when(ki == last_ki)
    def _():
        lam = _diff_lambda(lq_ref, lk_ref, lam_init)
        o = acc_ref[0] / l_ref[0] - lam * (acc_ref[1] / l_ref[1])
        o_ref[...] = _sub_norm(o, sg_ref[...], lam_init).astype(o_ref.dtype)


def _attn_prompt(q, k, v, lq, lk, sg, lam_init, *, batch, tq, tk):
    t, d = q.shape
    s = t // batch
    assert s % tq == 0 and s % tk == 0 and tq % CHUNK == 0 and tk % CHUNK == 0
    nq, nk = s // tq, s // tk

    def kv_map(b, h, qi, ki):
        last_ki = ((qi + 1) * tq - 1) // tk
        return (b * nk + jnp.minimum(ki, last_ki), h)

    return pl.pallas_call(
        functools.partial(_attn_prompt_kernel, lam_init=lam_init, tq=tq, tk=tk),
        out_shape=jax.ShapeDtypeStruct((t, d), BF16),
        grid=(batch, N_HEADS, nq, nk),
        in_specs=[
            pl.BlockSpec((2, HEAD_DIM), lambda b, h, qi, ki: (0, 0)),
            pl.BlockSpec((2, HEAD_DIM), lambda b, h, qi, ki: (0, 0)),
            pl.BlockSpec((1, HEAD_WIDTH), lambda b, h, qi, ki: (0, 0)),
            pl.BlockSpec((tq, HEAD_WIDTH), lambda b, h, qi, ki: (b * nq + qi, h)),
            pl.BlockSpec((tk, HEAD_WIDTH), kv_map),
            pl.BlockSpec((tk, HEAD_WIDTH), kv_map),
        ],
        out_specs=pl.BlockSpec((tq, HEAD_WIDTH), lambda b, h, qi, ki: (b * nq + qi, h)),
        scratch_shapes=[
            pltpu.VMEM((2, tq, 1), F32),
            pltpu.VMEM((2, tq, 1), F32),
            pltpu.VMEM((2, tq, HEAD_WIDTH), F32),
        ],
        compiler_params=_params("parallel", "parallel", "parallel", "arbitrary"),
        name="attn_prompt",
    )(lq, lk, sg, q, k, v)


def _attn_sample_kernel(lq_ref, lk_ref, sg_ref, q_ref, kn_ref, vn_ref, kc_ref, vc_ref, o_ref, *, lam_init):
    lam = _diff_lambda(lq_ref, lk_ref, lam_init)
    vc = vc_ref[...].astype(BF16)
    vn = vn_ref[...]
    outs = []
    for c in range(2):
        cols = slice(c * HEAD_DIM, (c + 1) * HEAD_DIM)
        q = q_ref[:, cols]
        s_c = _scores(q, kc_ref[:, cols].astype(BF16))
        s_n = _scores(q, kn_ref[:, cols])
        m = jnp.maximum(jnp.max(s_c, axis=1, keepdims=True), jnp.max(s_n, axis=1, keepdims=True))
        p_c = jnp.exp(s_c - m)
        p_n = jnp.exp(s_n - m)
        l = jnp.sum(p_c, axis=1, keepdims=True) + jnp.sum(p_n, axis=1, keepdims=True)
        acc = (jnp.dot(p_c.astype(BF16), vc, preferred_element_type=F32)
               + jnp.dot(p_n.astype(BF16), vn, preferred_element_type=F32))
        outs.append(acc / l)
    o = outs[0] - lam * outs[1]
    o_ref[...] = _sub_norm(o, sg_ref[...], lam_init).astype(o_ref.dtype)


def _attn_sample(q, k_new, v_new, cache_k, cache_v, lq, lk, sg, lam_init, *, batch):
    t, d = q.shape
    ts = t // batch
    past = cache_k.shape[0] // batch
    head = lambda b, h: (b, h)
    return pl.pallas_call(
        functools.partial(_attn_sample_kernel, lam_init=lam_init),
        out_shape=jax.ShapeDtypeStruct((t, d), BF16),
        grid=(batch, N_HEADS),
        in_specs=[
            pl.BlockSpec((2, HEAD_DIM), lambda b, h: (0, 0)),
            pl.BlockSpec((2, HEAD_DIM), lambda b, h: (0, 0)),
            pl.BlockSpec((1, HEAD_WIDTH), lambda b, h: (0, 0)),
            pl.BlockSpec((ts, HEAD_WIDTH), head),
            pl.BlockSpec((ts, HEAD_WIDTH), head),
            pl.BlockSpec((ts, HEAD_WIDTH), head),
            pl.BlockSpec((past, HEAD_WIDTH), head),
            pl.BlockSpec((past, HEAD_WIDTH), head),
        ],
        out_specs=pl.BlockSpec((ts, HEAD_WIDTH), head),
        compiler_params=_params("parallel", "parallel"),
        name="attn_sample",
    )(lq, lk, sg, q, k_new, v_new, cache_k, cache_v)


def _conv_gate_kernel(b_ref, c_ref, xt_ref, w_ref, prev_ref, z_ref, state_ref, g_ref, *, ts):
    si = pl.program_id(1)
    ns = pl.num_programs(1)
    lo = CARRY_ROWS

    @pl.when(si == 0)
    def _():
        g_ref[lo - (CONV_WIDTH - 1):lo, :] = prev_ref[...]

    @pl.when(si > 0)
    def _():
        g_ref[0:lo, :] = g_ref[ts:ts + lo, :]

    g_ref[lo:lo + ts, :] = c_ref[...] * xt_ref[...]
    y = (w_ref[0:1, :] * g_ref[lo - 2:lo - 2 + ts, :]
         + w_ref[1:2, :] * g_ref[lo - 1:lo - 1 + ts, :]
         + w_ref[2:3, :] * g_ref[lo:lo + ts, :])
    z_ref[...] = (b_ref[...] * y).astype(z_ref.dtype)

    @pl.when(si == ns - 1)
    def _():
        state_ref[...] = g_ref[lo + ts - (CONV_WIDTH - 1):lo + ts, :]


def _conv_gate(b, c, xt, w, prev, layer, *, batch, ts):
    t, d = b.shape
    s = t // batch
    assert s % ts == 0 and ts >= CARRY_ROWS
    ns = s // ts
    tile = pl.BlockSpec((ts, d), lambda bi, si: (bi * ns + si, 0))
    state = pl.BlockSpec((None, CONV_WIDTH - 1, d), lambda bi, si: (bi, 0, 0))
    return pl.pallas_call(
        functools.partial(_conv_gate_kernel, ts=ts),
        out_shape=[jax.ShapeDtypeStruct((t, d), BF16),
                   jax.ShapeDtypeStruct((batch, CONV_WIDTH - 1, d), F32)],
        grid=(batch, ns),
        in_specs=[tile, tile, tile,
                  pl.BlockSpec((None, CONV_WIDTH, d), lambda bi, si: (layer, 0, 0)),
                  state],
        out_specs=[tile, state],
        scratch_shapes=[pltpu.VMEM((CARRY_ROWS + ts, d), F32)],
        compiler_params=_params("parallel", "arbitrary"),
        name="conv_gate",
    )(b, c, xt, w, prev)


def _final_norm_kernel(x_ref, g_ref, o_ref):
    o_ref[...] = _rms_rows(x_ref[...], g_ref[...])


def _final_norm(x, g, *, tm):
    t, d = x.shape
    return pl.pallas_call(
        _final_norm_kernel,
        out_shape=jax.ShapeDtypeStruct((t, d), F32),
        grid=(t // tm,),
        in_specs=[pl.BlockSpec((tm, d), lambda i: (i, 0)), pl.BlockSpec((1, d), lambda i: (0, 0))],
        out_specs=pl.BlockSpec((tm, d), lambda i: (i, 0)),
        compiler_params=_params("parallel"),
        name="final_norm",
    )(x, g)


def _tiles(tokens, seq):
    return dict(
        ffn_tm=min(tokens, 512),
        ffn_tf=512,
        proj_tm=min(tokens, 512),
        attn_t=min(seq, 512),
        conv_ts=min(seq, 512),
    )


def _trunk(x, batch, caches, conv_states, p):
    b, seq, d = x.shape
    t = b * seq
    x = x.reshape(t, d)
    tl = _tiles(t, seq)
    depth = p["norm_g"].shape[0]
    new_states = []
    for i in range(depth):
        j = i // 2
        norm = lambda n: p["norm_g"][i, n].reshape(1, d)
        x = _ffn(x, norm(0), p["ffn_w_gu"], p["ffn_w_down"], i, 0, tm=tl["ffn_tm"], tf=tl["ffn_tf"])
        if i % 2 == 0:
            lam_init = 0.8 - 0.6 * math.exp(-0.3 * i)
            (q,) = _norm_proj(x, norm(1), p["attn_w_qkv"], j, 0, [BF16], tm=tl["proj_tm"])
            k, kb = _norm_proj(x, norm(1), p["attn_w_qkv"], j, 1, [F32, BF16], tm=tl["proj_tm"])
            v, vb = _norm_proj(x, norm(1), p["attn_w_qkv"], j, 2, [F32, BF16], tm=tl["proj_tm"])
            lq, lk = p["attn_lambda_q"][j], p["attn_lambda_k"][j]
            sg = p["attn_subln_g"][j].reshape(1, HEAD_WIDTH)
            if caches is None:
                o = _attn_prompt(q, kb, vb, lq, lk, sg, lam_init, batch=b, tq=tl["attn_t"], tk=tl["attn_t"])
            else:
                ck, cv = caches[j]
                o = _attn_sample(q, kb, vb, ck.reshape(-1, d), cv.reshape(-1, d), lq, lk, sg, lam_init,
                                 batch=b)
            x = _proj_residual(x, o, p["attn_w_o"], j, tm=tl["proj_tm"])
            new_states += [k.reshape(b, seq, 2 * N_HEADS, HEAD_DIM), v.reshape(b, seq, N_HEADS, HEAD_WIDTH)]
        else:
            (bg,) = _norm_proj(x, norm(1), p["conv_w_in"], j, 0, [F32], tm=tl["proj_tm"])
            (cg,) = _norm_proj(x, norm(1), p["conv_w_in"], j, 1, [F32], tm=tl["proj_tm"])
            (xt,) = _norm_proj(x, norm(1), p["conv_w_in"], j, 2, [F32], tm=tl["proj_tm"])
            prev = (jnp.zeros((b, CONV_WIDTH - 1, d), F32) if conv_states is None else conv_states[j])
            z, state = _conv_gate(bg, cg, xt, p["conv_w"], prev, j, batch=b, ts=tl["conv_ts"])
            x = _proj_residual(x, z, p["conv_w_out"], j, tm=tl["proj_tm"])
            new_states += [state]
        x = _ffn(x, norm(2), p["ffn_w_gu"], p["ffn_w_down"], i, 1, tm=tl["ffn_tm"], tf=tl["ffn_tf"])
    y = _final_norm(x, p["final_norm_g"].reshape(1, d), tm=tl["proj_tm"])
    return y.reshape(b, seq, d), new_states


def kernel(x_prompt, x_sample, cache_k_l0, cache_v_l0, state_conv_l1, cache_k_l2, cache_v_l2, state_conv_l3,
           norm_g, final_norm_g, ffn_w_gu, ffn_w_down, attn_w_qkv, attn_w_o, attn_lambda_q, attn_lambda_k,
           attn_subln_g, conv_w_in, conv_w, conv_w_out):
    p = dict(
        norm_g=norm_g, final_norm_g=final_norm_g,
        ffn_w_gu=ffn_w_gu.astype(BF16), ffn_w_down=ffn_w_down.astype(BF16),
        attn_w_qkv=attn_w_qkv.astype(BF16), attn_w_o=attn_w_o.astype(BF16),
        attn_lambda_q=attn_lambda_q, attn_lambda_k=attn_lambda_k, attn_subln_g=attn_subln_g,
        conv_w_in=conv_w_in.astype(BF16), conv_w=conv_w, conv_w_out=conv_w_out.astype(BF16),
    )
    y_p, new_p = _trunk(x_prompt, x_prompt.shape[0], None, None, p)
    y_s, new_s = _trunk(x_sample, x_sample.shape[0],
                        ((cache_k_l0, cache_v_l0), (cache_k_l2, cache_v_l2)),
                        (state_conv_l1, state_conv_l3), p)
    return (y_p, y_s, *new_p, *new_s)
```

```python
import functools
import math

import jax
import jax.numpy as jnp
from jax import lax
from jax.experimental import pallas as pl
from jax.experimental.pallas import tpu as pltpu

F32 = jnp.float32
BF16 = jnp.bfloat16

EPS = 1e-5
N_HEADS = 8
HEAD_DIM = 128
HEAD_WIDTH = 2 * HEAD_DIM
QUERY_SCALE = HEAD_DIM ** -0.5 * math.log2(math.e)
CHUNK = 64
CONV_WIDTH = 3
CARRY_ROWS = 8
LANES = 128
ROW_CHUNK = 32

V7X_VMEM_BYTES = 64 * 1024 * 1024
VMEM_LIMIT = V7X_VMEM_BYTES - 4 * 1024 * 1024


def _params(*semantics):
    return pltpu.CompilerParams(dimension_semantics=semantics, vmem_limit_bytes=VMEM_LIMIT)


def _rms_rows(x, g):
    ms = jnp.mean(x * x, axis=-1, keepdims=True)
    return x * lax.rsqrt(ms + EPS) * g


def _ffn_kernel(x_ref, g_ref, wg_ref, wu_ref, wd_ref, o_ref, h_ref):
    f = pl.program_id(1)

    @pl.when(f == 0)
    def _():
        x = x_ref[...]
        h_ref[...] = _rms_rows(x, g_ref[...]).astype(BF16)
        o_ref[...] = x

    h = h_ref[...]
    gate = jnp.dot(h, wg_ref[...], preferred_element_type=F32)
    up = jnp.dot(h, wu_ref[...], preferred_element_type=F32)
    a = (gate * jax.nn.sigmoid(gate) * up * 0.5).astype(BF16)
    o_ref[...] += jnp.dot(a, wd_ref[...], preferred_element_type=F32)


def _ffn(x, g, w_gu, w_down, layer, half, *, tm, tf):
    t, d = x.shape
    d_ff = w_down.shape[2]
    nf = d_ff // tf
    assert t % tm == 0 and d_ff % tf == 0
    return pl.pallas_call(
        _ffn_kernel,
        out_shape=jax.ShapeDtypeStruct((t, d), F32),
        grid=(t // tm, nf),
        in_specs=[
            pl.BlockSpec((tm, d), lambda i, f: (i, 0)),
            pl.BlockSpec((1, d), lambda i, f: (0, 0)),
            pl.BlockSpec((None, None, d, tf), lambda i, f: (layer, half, 0, f)),
            pl.BlockSpec((None, None, d, tf), lambda i, f: (layer, half, 0, f + nf)),
            pl.BlockSpec((None, None, tf, d), lambda i, f: (layer, half, f, 0)),
        ],
        out_specs=pl.BlockSpec((tm, d), lambda i, f: (i, 0)),
        scratch_shapes=[pltpu.VMEM((tm, d), BF16)],
        compiler_params=_params("parallel", "arbitrary"),
        name="ffn",
    )(x, g, w_gu, w_gu, w_down)


def _norm_proj_kernel(x_ref, g_ref, w_ref, *o_refs, scale):
    h = _rms_rows(x_ref[...], g_ref[...]).astype(BF16)
    y = jnp.dot(h, w_ref[...], preferred_element_type=F32)
    if scale != 1.0:
        y = y * scale
    for o_ref in o_refs:
        o_ref[...] = y.astype(o_ref.dtype)


def _norm_proj(x, g, w, layer, col, out_dtypes, *, tm, scale=1.0):
    t, d = x.shape
    assert t % tm == 0
    outs = pl.pallas_call(
        functools.partial(_norm_proj_kernel, scale=scale),
        out_shape=[jax.ShapeDtypeStruct((t, d), dt) for dt in out_dtypes],
        grid=(t // tm,),
        in_specs=[
            pl.BlockSpec((tm, d), lambda i: (i, 0)),
            pl.BlockSpec((1, d), lambda i: (0, 0)),
            pl.BlockSpec((None, d, d), lambda i: (layer, 0, col)),
        ],
        out_specs=[pl.BlockSpec((tm, d), lambda i: (i, 0)) for _ in out_dtypes],
        compiler_params=_params("parallel"),
        name="norm_proj",
    )(x, g, w)
    return outs


def _proj_residual_kernel(x_ref, a_ref, w_ref, o_ref):
    o_ref[...] = x_ref[...] + jnp.dot(a_ref[...], w_ref[...], preferred_element_type=F32)


def _proj_residual(x, a, w, layer, *, tm):
    t, d = x.shape
    assert t % tm == 0
    return pl.pallas_call(
        _proj_residual_kernel,
        out_shape=jax.ShapeDtypeStruct((t, d), F32),
        grid=(t // tm,),
        in_specs=[
            pl.BlockSpec((tm, d), lambda i: (i, 0)),
            pl.BlockSpec((tm, d), lambda i: (i, 0)),
            pl.BlockSpec((None, d, d), lambda i: (layer, 0, 0)),
        ],
        out_specs=pl.BlockSpec((tm, d), lambda i: (i, 0)),
        compiler_params=_params("parallel"),
        name="proj_residual",
    )(x, a, w)


def _diff_lambda(lq_ref, lk_ref, lam_init):
    e = jnp.exp(jnp.sum(lq_ref[...] * lk_ref[...], axis=1, keepdims=True))
    return e[0:1] - e[1:2] + lam_init


def _sub_norm(o, sg, lam_init):
    return _rms_rows(o, sg) * (1.0 - lam_init)


def _scores(q, k):
    return lax.dot_general(q, k, (((1,), (1,)), ((), ())), preferred_element_type=F32)


def _attn_prompt_kernel(lq_ref, lk_ref, sg_ref, q_ref, k_ref, v_ref, o_ref,
                        sa_ref, sb_ref, p_ref, m_ref, l_ref, acc_ref, *, lam_init, t):
    qi = pl.program_id(2)
    m_ref[...] = jnp.full(m_ref.shape, -jnp.inf, F32)
    l_ref[...] = jnp.zeros(l_ref.shape, F32)
    acc_ref[...] = jnp.zeros(acc_ref.shape, F32)

    def key_rows(j):
        return pl.ds(pl.multiple_of(j * t, t), t)

    def form_scores(j, s_ref):
        keys = key_rows(j)
        for c in range(2):
            cols = slice(c * HEAD_DIM, (c + 1) * HEAD_DIM)
            s_ref[c] = _scores(q_ref[:, cols], k_ref[keys, cols])

    def softmax_pv(j, s_ref, diagonal):
        v = v_ref[key_rows(j), :]
        if diagonal:
            r_chunk = lax.broadcasted_iota(jnp.int32, (t, 1), 0) // CHUNK
            k_chunk = lax.broadcasted_iota(jnp.int32, (1, t), 1) // CHUNK
            visible = k_chunk <= r_chunk
        for c in range(2):
            if diagonal:
                s_ref[c] = jnp.where(visible, s_ref[c], -jnp.inf)
            m_prev = m_ref[c]
            m_new = jnp.maximum(m_prev, jnp.max(s_ref[c], axis=1, keepdims=True))
            alpha = jnp.exp2(m_prev - m_new)
            for r in range(0, t, ROW_CHUNK):
                rows = slice(r, r + ROW_CHUNK)
                pr = jnp.exp2(s_ref[c, rows, :] - pltpu.repeat(m_new[rows], t // LANES, axis=1))
                l_ref[c, rows, :] = (alpha[rows] * l_ref[c, rows, :]
                                     + sum(pr[:, i * LANES:(i + 1) * LANES] for i in range(t // LANES)))
                p_ref[c, rows, :] = pr.astype(BF16)
            acc_ref[c] = (pltpu.repeat(alpha, HEAD_WIDTH // LANES, axis=1) * acc_ref[c]
                          + jnp.dot(p_ref[c], v, preferred_element_type=F32))
            m_ref[c] = m_new

    following = lambda u: jnp.minimum(u + 1, jnp.maximum(qi - 1, 0))

    form_scores(qi, sa_ref)
    form_scores(0, sb_ref)
    softmax_pv(qi, sa_ref, True)

    def pair(i, carry):
        u = 2 * i
        form_scores(following(u), sa_ref)
        softmax_pv(u, sb_ref, False)
        form_scores(following(u + 1), sb_ref)
        softmax_pv(u + 1, sa_ref, False)
        return carry

    lax.fori_loop(0, qi // 2, pair, 0)

    @pl.when(qi % 2 == 1)
    def _():
        softmax_pv(qi - 1, sb_ref, False)

    lam = _diff_lambda(lq_ref, lk_ref, lam_init)
    inv = [1.0 / jnp.sum(l_ref[c], axis=1, keepdims=True) for c in range(2)]
    o = acc_ref[0] * inv[0] - lam * (acc_ref[1] * inv[1])
    o_ref[...] = _sub_norm(o, sg_ref[...], lam_init).astype(o_ref.dtype)


def _attn_prompt(q, k, v, lq, lk, sg, lam_init, *, batch, t):
    tokens, d = q.shape
    s = tokens // batch
    assert s % t == 0 and t % CHUNK == 0 and t % LANES == 0 and t % ROW_CHUNK == 0
    nq = s // t
    const = lambda b, h, qi: (0, 0)
    stream_head = lambda b, h, qi: (b, h)
    q_block = lambda b, h, qi: (b * nq + qi, h)
    resident = dict(pipeline_mode=pl.Buffered(1))
    return pl.pallas_call(
        functools.partial(_attn_prompt_kernel, lam_init=lam_init, t=t),
        out_shape=jax.ShapeDtypeStruct((tokens, d), BF16),
        grid=(batch, N_HEADS, nq),
        in_specs=[
            pl.BlockSpec((2, HEAD_DIM), const),
            pl.BlockSpec((2, HEAD_DIM), const),
            pl.BlockSpec((1, HEAD_WIDTH), const),
            pl.BlockSpec((t, HEAD_WIDTH), q_block),
            pl.BlockSpec((s, HEAD_WIDTH), stream_head, **resident),
            pl.BlockSpec((s, HEAD_WIDTH), stream_head, **resident),
        ],
        out_specs=pl.BlockSpec((t, HEAD_WIDTH), q_block),
        scratch_shapes=[
            pltpu.VMEM((2, t, t), F32),
            pltpu.VMEM((2, t, t), F32),
            pltpu.VMEM((2, t, t), BF16),
            pltpu.VMEM((2, t, LANES), F32),
            pltpu.VMEM((2, t, LANES), F32),
            pltpu.VMEM((2, t, HEAD_WIDTH), F32),
        ],
        compiler_params=_params("parallel", "parallel", "arbitrary"),
        name="attn_prompt",
    )(lq, lk, sg, q, k, v)


def _attn_sample_kernel(lq_ref, lk_ref, sg_ref, q_ref, kn_ref, vn_ref, kc_ref, vc_ref, o_ref, *, lam_init):
    lam = _diff_lambda(lq_ref, lk_ref, lam_init)
    vc = vc_ref[...].astype(BF16)
    vn = vn_ref[...]
    outs = []
    for c in range(2):
        cols = slice(c * HEAD_DIM, (c + 1) * HEAD_DIM)
        q = q_ref[:, cols]
        s_c = _scores(q, kc_ref[:, cols].astype(BF16))
        s_n = _scores(q, kn_ref[:, cols])
        m = jnp.maximum(jnp.max(s_c, axis=1, keepdims=True), jnp.max(s_n, axis=1, keepdims=True))
        p_c = jnp.exp2(s_c - m)
        p_n = jnp.exp2(s_n - m)
        l = jnp.sum(p_c, axis=1, keepdims=True) + jnp.sum(p_n, axis=1, keepdims=True)
        acc = (jnp.dot(p_c.astype(BF16), vc, preferred_element_type=F32)
               + jnp.dot(p_n.astype(BF16), vn, preferred_element_type=F32))
        outs.append(acc / l)
    o = outs[0] - lam * outs[1]
    o_ref[...] = _sub_norm(o, sg_ref[...], lam_init).astype(o_ref.dtype)


def _attn_sample(q, k_new, v_new, cache_k, cache_v, lq, lk, sg, lam_init, *, batch):
    t, d = q.shape
    ts = t // batch
    past = cache_k.shape[0] // batch
    head = lambda b, h: (b, h)
    return pl.pallas_call(
        functools.partial(_attn_sample_kernel, lam_init=lam_init),
        out_shape=jax.ShapeDtypeStruct((t, d), BF16),
        grid=(batch, N_HEADS),
        in_specs=[
            pl.BlockSpec((2, HEAD_DIM), lambda b, h: (0, 0)),
            pl.BlockSpec((2, HEAD_DIM), lambda b, h: (0, 0)),
            pl.BlockSpec((1, HEAD_WIDTH), lambda b, h: (0, 0)),
            pl.BlockSpec((ts, HEAD_WIDTH), head),
            pl.BlockSpec((ts, HEAD_WIDTH), head),
            pl.BlockSpec((ts, HEAD_WIDTH), head),
            pl.BlockSpec((past, HEAD_WIDTH), head),
            pl.BlockSpec((past, HEAD_WIDTH), head),
        ],
        out_specs=pl.BlockSpec((ts, HEAD_WIDTH), head),
        compiler_params=_params("parallel", "parallel"),
        name="attn_sample",
    )(lq, lk, sg, q, k_new, v_new, cache_k, cache_v)


def _conv_gate_kernel(b_ref, c_ref, xt_ref, w_ref, prev_ref, z_ref, state_ref, g_ref, *, ts):
    si = pl.program_id(1)
    ns = pl.num_programs(1)
    lo = CARRY_ROWS

    @pl.when(si == 0)
    def _():
        g_ref[lo - (CONV_WIDTH - 1):lo, :] = prev_ref[...]

    @pl.when(si > 0)
    def _():
        g_ref[0:lo, :] = g_ref[ts:ts + lo, :]

    g_ref[lo:lo + ts, :] = c_ref[...] * xt_ref[...]
    y = (w_ref[0:1, :] * g_ref[lo - 2:lo - 2 + ts, :]
         + w_ref[1:2, :] * g_ref[lo - 1:lo - 1 + ts, :]
         + w_ref[2:3, :] * g_ref[lo:lo + ts, :])
    z_ref[...] = (b_ref[...] * y).astype(z_ref.dtype)

    @pl.when(si == ns - 1)
    def _():
        state_ref[...] = g_ref[lo + ts - (CONV_WIDTH - 1):lo + ts, :]


def _conv_gate(b, c, xt, w, prev, layer, *, batch, ts):
    t, d = b.shape
    s = t // batch
    assert s % ts == 0 and ts >= CARRY_ROWS
    ns = s // ts
    tile = pl.BlockSpec((ts, d), lambda bi, si: (bi * ns + si, 0))
    state = pl.BlockSpec((None, CONV_WIDTH - 1, d), lambda bi, si: (bi, 0, 0))
    return pl.pallas_call(
        functools.partial(_conv_gate_kernel, ts=ts),
        out_shape=[jax.ShapeDtypeStruct((t, d), BF16),
                   jax.ShapeDtypeStruct((batch, CONV_WIDTH - 1, d), F32)],
        grid=(batch, ns),
        in_specs=[tile, tile, tile,
                  pl.BlockSpec((None, CONV_WIDTH, d), lambda bi, si: (layer, 0, 0)),
                  state],
        out_specs=[tile, state],
        scratch_shapes=[pltpu.VMEM((CARRY_ROWS + ts, d), F32)],
        compiler_params=_params("parallel", "arbitrary"),
        name="conv_gate",
    )(b, c, xt, w, prev)


def _final_norm_kernel(x_ref, g_ref, o_ref):
    o_ref[...] = _rms_rows(x_ref[...], g_ref[...])


def _final_norm(x, g, *, tm):
    t, d = x.shape
    return pl.pallas_call(
        _final_norm_kernel,
        out_shape=jax.ShapeDtypeStruct((t, d), F32),
        grid=(t // tm,),
        in_specs=[pl.BlockSpec((tm, d), lambda i: (i, 0)), pl.BlockSpec((1, d), lambda i: (0, 0))],
        out_specs=pl.BlockSpec((tm, d), lambda i: (i, 0)),
        compiler_params=_params("parallel"),
        name="final_norm",
    )(x, g)


def _tiles(tokens, seq):
    return dict(
        ffn_tm=min(tokens, 1024),
        ffn_tf=512,
        proj_tm=min(tokens, 512),
        attn_t=min(seq, 1024),
        conv_ts=min(seq, 512),
    )


def _trunk(x, batch, caches, conv_states, p):
    b, seq, d = x.shape
    t = b * seq
    x = x.reshape(t, d)
    tl = _tiles(t, seq)
    depth = p["norm_g"].shape[0]
    new_states = []
    for i in range(depth):
        j = i // 2
        norm = lambda n: p["norm_g"][i, n].reshape(1, d)
        x = _ffn(x, norm(0), p["ffn_w_gu"], p["ffn_w_down"], i, 0, tm=tl["ffn_tm"], tf=tl["ffn_tf"])
        if i % 2 == 0:
            lam_init = 0.8 - 0.6 * math.exp(-0.3 * i)
            (q,) = _norm_proj(x, norm(1), p["attn_w_qkv"], j, 0, [BF16], tm=tl["proj_tm"],
                              scale=QUERY_SCALE)
            k, kb = _norm_proj(x, norm(1), p["attn_w_qkv"], j, 1, [F32, BF16], tm=tl["proj_tm"])
            v, vb = _norm_proj(x, norm(1), p["attn_w_qkv"], j, 2, [F32, BF16], tm=tl["proj_tm"])
            lq, lk = p["attn_lambda_q"][j], p["attn_lambda_k"][j]
            sg = p["attn_subln_g"][j].reshape(1, HEAD_WIDTH)
            if caches is None:
                o = _attn_prompt(q, kb, vb, lq, lk, sg, lam_init, batch=b, t=tl["attn_t"])
            else:
                ck, cv = caches[j]
                o = _attn_sample(q, kb, vb, ck.reshape(-1, d), cv.reshape(-1, d), lq, lk, sg, lam_init,
                                 batch=b)
            x = _proj_residual(x, o, p["attn_w_o"], j, tm=tl["proj_tm"])
            new_states += [k.reshape(b, seq, 2 * N_HEADS, HEAD_DIM), v.reshape(b, seq, N_HEADS, HEAD_WIDTH)]
        else:
            (bg,) = _norm_proj(x, norm(1), p["conv_w_in"], j, 0, [F32], tm=tl["proj_tm"])
            (cg,) = _norm_proj(x, norm(1), p["conv_w_in"], j, 1, [F32], tm=tl["proj_tm"])
            (xt,) = _norm_proj(x, norm(1), p["conv_w_in"], j, 2, [F32], tm=tl["proj_tm"])
            prev = (jnp.zeros((b, CONV_WIDTH - 1, d), F32) if conv_states is None else conv_states[j])
            z, state = _conv_gate(bg, cg, xt, p["conv_w"], prev, j, batch=b, ts=tl["conv_ts"])
            x = _proj_residual(x, z, p["conv_w_out"], j, tm=tl["proj_tm"])
            new_states += [state]
        x = _ffn(x, norm(2), p["ffn_w_gu"], p["ffn_w_down"], i, 1, tm=tl["ffn_tm"], tf=tl["ffn_tf"])
    y = _final_norm(x, p["final_norm_g"].reshape(1, d), tm=tl["proj_tm"])
    return y.reshape(b, seq, d), new_states


def kernel(x_prompt, x_sample, cache_k_l0, cache_v_l0, state_conv_l1, cache_k_l2, cache_v_l2, state_conv_l3,
           norm_g, final_norm_g, ffn_w_gu, ffn_w_down, attn_w_qkv, attn_w_o, attn_lambda_q, attn_lambda_k,
           attn_subln_g, conv_w_in, conv_w, conv_w_out):
    p = dict(
        norm_g=norm_g, final_norm_g=final_norm_g,
        ffn_w_gu=ffn_w_gu.astype(BF16), ffn_w_down=ffn_w_down.astype(BF16),
        attn_w_qkv=attn_w_qkv.astype(BF16), attn_w_o=attn_w_o.astype(BF16),
        attn_lambda_q=attn_lambda_q, attn_lambda_k=attn_lambda_k, attn_subln_g=attn_subln_g,
        conv_w_in=conv_w_in.astype(BF16), conv_w=conv_w, conv_w_out=conv_w_out.astype(BF16),
    )
    y_p, new_p = _trunk(x_prompt, x_prompt.shape[0], None, None, p)
    y_s, new_s = _trunk(x_sample, x_sample.shape[0],
                        ((cache_k_l0, cache_v_l0), (cache_k_l2, cache_v_l2)),
                        (state_conv_l1, state_conv_l3), p)
    return (y_p, y_s, *new_p, *new_s)
```

```python
import functools
import math

import jax
import jax.numpy as jnp
from jax import lax
from jax.experimental import pallas as pl
from jax.experimental.pallas import tpu as pltpu

F32 = jnp.float32
BF16 = jnp.bfloat16

EPS = 1e-5
N_HEADS = 8
HEAD_DIM = 128
HEAD_WIDTH = 2 * HEAD_DIM
QUERY_SCALE = HEAD_DIM ** -0.5 * math.log2(math.e)
CHUNK = 64
CONV_WIDTH = 3
CARRY_ROWS = 8
LANES = 128
ROW_CHUNK = 32

V7X_VMEM_BYTES = 64 * 1024 * 1024
VMEM_LIMIT = V7X_VMEM_BYTES - 4 * 1024 * 1024


def _params(*semantics):
    return pltpu.CompilerParams(dimension_semantics=semantics, vmem_limit_bytes=VMEM_LIMIT)


def _rms_rows(x, g):
    ms = jnp.mean(x * x, axis=-1, keepdims=True)
    return x * lax.rsqrt(ms + EPS) * g


def _ffn_kernel(x_ref, g_ref, wg_ref, wu_ref, wd_ref, o_ref, h_ref):
    f = pl.program_id(1)

    @pl.when(f == 0)
    def _():
        x = x_ref[...]
        h_ref[...] = _rms_rows(x, g_ref[...]).astype(BF16)
        o_ref[...] = x

    h = h_ref[...]
    gate = jnp.dot(h, wg_ref[...], preferred_element_type=F32)
    up = jnp.dot(h, wu_ref[...], preferred_element_type=F32)
    a = (gate * jax.nn.sigmoid(gate) * up * 0.5).astype(BF16)
    o_ref[...] += jnp.dot(a, wd_ref[...], preferred_element_type=F32)


def _ffn(x, g, w_gu, w_down, layer, half, *, tm, tf):
    t, d = x.shape
    d_ff = w_down.shape[2]
    nf = d_ff // tf
    assert t % tm == 0 and d_ff % tf == 0
    return pl.pallas_call(
        _ffn_kernel,
        out_shape=jax.ShapeDtypeStruct((t, d), F32),
        grid=(t // tm, nf),
        in_specs=[
            pl.BlockSpec((tm, d), lambda i, f: (i, 0)),
            pl.BlockSpec((1, d), lambda i, f: (0, 0)),
            pl.BlockSpec((None, None, d, tf), lambda i, f: (layer, half, 0, f)),
            pl.BlockSpec((None, None, d, tf), lambda i, f: (layer, half, 0, f + nf)),
            pl.BlockSpec((None, None, tf, d), lambda i, f: (layer, half, f, 0)),
        ],
        out_specs=pl.BlockSpec((tm, d), lambda i, f: (i, 0)),
        scratch_shapes=[pltpu.VMEM((tm, d), BF16)],
        compiler_params=_params("parallel", "arbitrary"),
        name="ffn",
    )(x, g, w_gu, w_gu, w_down)


def _norm_proj_kernel(x_ref, g_ref, w_ref, *o_refs, scale):
    h = _rms_rows(x_ref[...], g_ref[...]).astype(BF16)
    y = jnp.dot(h, w_ref[...], preferred_element_type=F32)
    if scale != 1.0:
        y = y * scale
    for o_ref in o_refs:
        o_ref[...] = y.astype(o_ref.dtype)


def _norm_proj(x, g, w, layer, col, out_dtypes, *, tm, scale=1.0):
    t, d = x.shape
    assert t % tm == 0
    outs = pl.pallas_call(
        functools.partial(_norm_proj_kernel, scale=scale),
        out_shape=[jax.ShapeDtypeStruct((t, d), dt) for dt in out_dtypes],
        grid=(t // tm,),
        in_specs=[
            pl.BlockSpec((tm, d), lambda i: (i, 0)),
            pl.BlockSpec((1, d), lambda i: (0, 0)),
            pl.BlockSpec((None, d, d), lambda i: (layer, 0, col)),
        ],
        out_specs=[pl.BlockSpec((tm, d), lambda i: (i, 0)) for _ in out_dtypes],
        compiler_params=_params("parallel"),
        name="norm_proj",
    )(x, g, w)
    return outs


def _proj_residual_kernel(x_ref, a_ref, w_ref, o_ref):
    o_ref[...] = x_ref[...] + jnp.dot(a_ref[...], w_ref[...], preferred_element_type=F32)


def _proj_residual(x, a, w, layer, *, tm):
    t, d = x.shape
    assert t % tm == 0
    return pl.pallas_call(
        _proj_residual_kernel,
        out_shape=jax.ShapeDtypeStruct((t, d), F32),
        grid=(t // tm,),
        in_specs=[
            pl.BlockSpec((tm, d), lambda i: (i, 0)),
            pl.BlockSpec((tm, d), lambda i: (i, 0)),
            pl.BlockSpec((None, d, d), lambda i: (layer, 0, 0)),
        ],
        out_specs=pl.BlockSpec((tm, d), lambda i: (i, 0)),
        compiler_params=_params("parallel"),
        name="proj_residual",
    )(x, a, w)


def _diff_lambda(lq_ref, lk_ref, lam_init):
    e = jnp.exp(jnp.sum(lq_ref[...] * lk_ref[...], axis=1, keepdims=True))
    return e[0:1] - e[1:2] + lam_init


def _sub_norm(o, sg, lam_init):
    return _rms_rows(o, sg) * (1.0 - lam_init)


def _scores(q, k):
    return lax.dot_general(q, k, (((1,), (1,)), ((), ())), preferred_element_type=F32)


def _attn_prompt_kernel(lq_ref, lk_ref, sg_ref, q_ref, k_ref, v_ref, o_ref,
                        s_ref, pa_ref, pb_ref, aa_ref, ab_ref, m_ref, l_ref, acc_ref, *, lam_init, t):
    qi = pl.program_id(2)
    m_ref[...] = jnp.full(m_ref.shape, -jnp.inf, F32)
    l_ref[...] = jnp.zeros(l_ref.shape, F32)
    acc_ref[...] = jnp.zeros(acc_ref.shape, F32)

    def key_rows(j):
        return pl.ds(pl.multiple_of(j * t, t), t)

    def scores_softmax(j, p_ref, a_ref, diagonal):
        keys = key_rows(j)
        if diagonal:
            r_chunk = lax.broadcasted_iota(jnp.int32, (t, 1), 0) // CHUNK
            k_chunk = lax.broadcasted_iota(jnp.int32, (1, t), 1) // CHUNK
            visible = k_chunk <= r_chunk
        for c in range(2):
            cols = slice(c * HEAD_DIM, (c + 1) * HEAD_DIM)
            s = _scores(q_ref[:, cols], k_ref[keys, cols])
            s_ref[c] = jnp.where(visible, s, -jnp.inf) if diagonal else s
        for c in range(2):
            m_prev = m_ref[c]
            m_new = jnp.maximum(m_prev, jnp.max(s_ref[c], axis=1, keepdims=True))
            alpha = jnp.exp2(m_prev - m_new)
            a_ref[c] = alpha
            m_ref[c] = m_new
            for r in range(0, t, ROW_CHUNK):
                rows = slice(r, r + ROW_CHUNK)
                pr = jnp.exp2(s_ref[c, rows, :] - jnp.tile(m_new[rows], (1, t // LANES)))
                l_ref[c, rows, :] = (alpha[rows] * l_ref[c, rows, :]
                                     + sum(pr[:, i * LANES:(i + 1) * LANES] for i in range(t // LANES)))
                p_ref[c, rows, :] = pr.astype(BF16)

    def value_product(j, p_ref, a_ref):
        v = v_ref[key_rows(j), :]
        for c in range(2):
            acc_ref[c] = (jnp.tile(a_ref[c], (1, HEAD_WIDTH // LANES)) * acc_ref[c]
                          + jnp.dot(p_ref[c], v, preferred_element_type=F32))

    scores_softmax(qi, pa_ref, aa_ref, True)

    def pair(i, carry):
        u = 2 * i
        scores_softmax(u, pb_ref, ab_ref, False)
        value_product(jnp.where(i == 0, qi, u - 1), pa_ref, aa_ref)
        scores_softmax(u + 1, pa_ref, aa_ref, False)
        value_product(u, pb_ref, ab_ref)
        return carry

    lax.fori_loop(0, qi // 2, pair, 0)

    @pl.when(qi % 2 == 1)
    def _():
        scores_softmax(qi - 1, pb_ref, ab_ref, False)
        value_product(jnp.where(qi == 1, qi, qi - 2), pa_ref, aa_ref)
        value_product(qi - 1, pb_ref, ab_ref)

    @pl.when(qi % 2 == 0)
    def _():
        value_product(jnp.where(qi == 0, qi, qi - 1), pa_ref, aa_ref)

    lam = _diff_lambda(lq_ref, lk_ref, lam_init)
    inv = [1.0 / jnp.sum(l_ref[c], axis=1, keepdims=True) for c in range(2)]
    o = acc_ref[0] * inv[0] - lam * (acc_ref[1] * inv[1])
    o_ref[...] = _sub_norm(o, sg_ref[...], lam_init).astype(o_ref.dtype)


def _attn_prompt(q, k, v, lq, lk, sg, lam_init, *, batch, t):
    tokens, d = q.shape
    s = tokens // batch
    assert s % t == 0 and t % CHUNK == 0 and t % LANES == 0 and t % ROW_CHUNK == 0
    nq = s // t
    const = lambda b, h, qi: (0, 0)
    stream_head = lambda b, h, qi: (b, h)
    q_block = lambda b, h, qi: (b * nq + qi, h)
    resident = dict(pipeline_mode=pl.Buffered(1))
    return pl.pallas_call(
        functools.partial(_attn_prompt_kernel, lam_init=lam_init, t=t),
        out_shape=jax.ShapeDtypeStruct((tokens, d), BF16),
        grid=(batch, N_HEADS, nq),
        in_specs=[
            pl.BlockSpec((2, HEAD_DIM), const),
            pl.BlockSpec((2, HEAD_DIM), const),
            pl.BlockSpec((1, HEAD_WIDTH), const),
            pl.BlockSpec((t, HEAD_WIDTH), q_block),
            pl.BlockSpec((s, HEAD_WIDTH), stream_head, **resident),
            pl.BlockSpec((s, HEAD_WIDTH), stream_head, **resident),
        ],
        out_specs=pl.BlockSpec((t, HEAD_WIDTH), q_block),
        scratch_shapes=[
            pltpu.VMEM((2, t, t), F32),
            pltpu.VMEM((2, t, t), BF16),
            pltpu.VMEM((2, t, t), BF16),
            pltpu.VMEM((2, t, LANES), F32),
            pltpu.VMEM((2, t, LANES), F32),
            pltpu.VMEM((2, t, LANES), F32),
            pltpu.VMEM((2, t, LANES), F32),
            pltpu.VMEM((2, t, HEAD_WIDTH), F32),
        ],
        compiler_params=_params("parallel", "parallel", "arbitrary"),
        name="attn_prompt",
    )(lq, lk, sg, q, k, v)


def _attn_sample_kernel(lq_ref, lk_ref, sg_ref, q_ref, kn_ref, vn_ref, kc_ref, vc_ref, o_ref, *, lam_init):
    lam = _diff_lambda(lq_ref, lk_ref, lam_init)
    vc = vc_ref[...].astype(BF16)
    vn = vn_ref[...]
    outs = []
    for c in range(2):
        cols = slice(c * HEAD_DIM, (c + 1) * HEAD_DIM)
        q = q_ref[:, cols]
        s_c = _scores(q, kc_ref[:, cols].astype(BF16))
        s_n = _scores(q, kn_ref[:, cols])
        m = jnp.maximum(jnp.max(s_c, axis=1, keepdims=True), jnp.max(s_n, axis=1, keepdims=True))
        p_c = jnp.exp2(s_c - m)
        p_n = jnp.exp2(s_n - m)
        l = jnp.sum(p_c, axis=1, keepdims=True) + jnp.sum(p_n, axis=1, keepdims=True)
        acc = (jnp.dot(p_c.astype(BF16), vc, preferred_element_type=F32)
               + jnp.dot(p_n.astype(BF16), vn, preferred_element_type=F32))
        outs.append(acc / l)
    o = outs[0] - lam * outs[1]
    o_ref[...] = _sub_norm(o, sg_ref[...], lam_init).astype(o_ref.dtype)


def _attn_sample(q, k_new, v_new, cache_k, cache_v, lq, lk, sg, lam_init, *, batch):
    t, d = q.shape
    ts = t // batch
    past = cache_k.shape[0] // batch
    head = lambda b, h: (b, h)
    return pl.pallas_call(
        functools.partial(_attn_sample_kernel, lam_init=lam_init),
        out_shape=jax.ShapeDtypeStruct((t, d), BF16),
        grid=(batch, N_HEADS),
        in_specs=[
            pl.BlockSpec((2, HEAD_DIM), lambda b, h: (0, 0)),
            pl.BlockSpec((2, HEAD_DIM), lambda b, h: (0, 0)),
            pl.BlockSpec((1, HEAD_WIDTH), lambda b, h: (0, 0)),
            pl.BlockSpec((ts, HEAD_WIDTH), head),
            pl.BlockSpec((ts, HEAD_WIDTH), head),
            pl.BlockSpec((ts, HEAD_WIDTH), head),
            pl.BlockSpec((past, HEAD_WIDTH), head),
            pl.BlockSpec((past, HEAD_WIDTH), head),
        ],
        out_specs=pl.BlockSpec((ts, HEAD_WIDTH), head),
        compiler_params=_params("parallel", "parallel"),
        name="attn_sample",
    )(lq, lk, sg, q, k_new, v_new, cache_k, cache_v)


def _conv_gate_kernel(b_ref, c_ref, xt_ref, w_ref, prev_ref, z_ref, state_ref, g_ref, *, ts):
    si = pl.program_id(1)
    ns = pl.num_programs(1)
    lo = CARRY_ROWS

    @pl.when(si == 0)
    def _():
        g_ref[lo - (CONV_WIDTH - 1):lo, :] = prev_ref[...]

    @pl.when(si > 0)
    def _():
        g_ref[0:lo, :] = g_ref[ts:ts + lo, :]

    g_ref[lo:lo + ts, :] = c_ref[...] * xt_ref[...]
    y = (w_ref[0:1, :] * g_ref[lo - 2:lo - 2 + ts, :]
         + w_ref[1:2, :] * g_ref[lo - 1:lo - 1 + ts, :]
         + w_ref[2:3, :] * g_ref[lo:lo + ts, :])
    z_ref[...] = (b_ref[...] * y).astype(z_ref.dtype)

    @pl.when(si == ns - 1)
    def _():
        state_ref[...] = g_ref[lo + ts - (CONV_WIDTH - 1):lo + ts, :]


def _conv_gate(b, c, xt, w, prev, layer, *, batch, ts):
    t, d = b.shape
    s = t // batch
    assert s % ts == 0 and ts >= CARRY_ROWS
    ns = s // ts
    tile = pl.BlockSpec((ts, d), lambda bi, si: (bi * ns + si, 0))
    state = pl.BlockSpec((None, CONV_WIDTH - 1, d), lambda bi, si: (bi, 0, 0))
    return pl.pallas_call(
        functools.partial(_conv_gate_kernel, ts=ts),
        out_shape=[jax.ShapeDtypeStruct((t, d), BF16),
                   jax.ShapeDtypeStruct((batch, CONV_WIDTH - 1, d), F32)],
        grid=(batch, ns),
        in_specs=[tile, tile, tile,
                  pl.BlockSpec((None, CONV_WIDTH, d), lambda bi, si: (layer, 0, 0)),
                  state],
        out_specs=[tile, state],
        scratch_shapes=[pltpu.VMEM((CARRY_ROWS + ts, d), F32)],
        compiler_params=_params("parallel", "arbitrary"),
        name="conv_gate",
    )(b, c, xt, w, prev)


def _final_norm_kernel(x_ref, g_ref, o_ref):
    o_ref[...] = _rms_rows(x_ref[...], g_ref[...])


def _final_norm(x, g, *, tm):
    t, d = x.shape
    return pl.pallas_call(
        _final_norm_kernel,
        out_shape=jax.ShapeDtypeStruct((t, d), F32),
        grid=(t // tm,),
        in_specs=[pl.BlockSpec((tm, d), lambda i: (i, 0)), pl.BlockSpec((1, d), lambda i: (0, 0))],
        out_specs=pl.BlockSpec((tm, d), lambda i: (i, 0)),
        compiler_params=_params("parallel"),
        name="final_norm",
    )(x, g)


def _tiles(tokens, seq):
    return dict(
        ffn_tm=min(tokens, 1024),
        ffn_tf=512,
        proj_tm=min(tokens, 512),
        attn_t=min(seq, 1024),
        conv_ts=min(seq, 512),
    )


def _trunk(x, batch, caches, conv_states, p):
    b, seq, d = x.shape
    t = b * seq
    x = x.reshape(t, d)
    tl = _tiles(t, seq)
    depth = p["norm_g"].shape[0]
    new_states = []
    for i in range(depth):
        j = i // 2
        norm = lambda n: p["norm_g"][i, n].reshape(1, d)
        x = _ffn(x, norm(0), p["ffn_w_gu"], p["ffn_w_down"], i, 0, tm=tl["ffn_tm"], tf=tl["ffn_tf"])
        if i % 2 == 0:
            lam_init = 0.8 - 0.6 * math.exp(-0.3 * i)
            (q,) = _norm_proj(x, norm(1), p["attn_w_qkv"], j, 0, [BF16], tm=tl["proj_tm"],
                              scale=QUERY_SCALE)
            k, kb = _norm_proj(x, norm(1), p["attn_w_qkv"], j, 1, [F32, BF16], tm=tl["proj_tm"])
            v, vb = _norm_proj(x, norm(1), p["attn_w_qkv"], j, 2, [F32, BF16], tm=tl["proj_tm"])
            lq, lk = p["attn_lambda_q"][j], p["attn_lambda_k"][j]
            sg = p["attn_subln_g"][j].reshape(1, HEAD_WIDTH)
            if caches is None:
                o = _attn_prompt(q, kb, vb, lq, lk, sg, lam_init, batch=b, t=tl["attn_t"])
            else:
                ck, cv = caches[j]
                o = _attn_sample(q, kb, vb, ck.reshape(-1, d), cv.reshape(-1, d), lq, lk, sg, lam_init,
                                 batch=b)
            x = _proj_residual(x, o, p["attn_w_o"], j, tm=tl["proj_tm"])
            new_states += [k.reshape(b, seq, 2 * N_HEADS, HEAD_DIM), v.reshape(b, seq, N_HEADS, HEAD_WIDTH)]
        else:
            (bg,) = _norm_proj(x, norm(1), p["conv_w_in"], j, 0, [F32], tm=tl["proj_tm"])
            (cg,) = _norm_proj(x, norm(1), p["conv_w_in"], j, 1, [F32], tm=tl["proj_tm"])
            (xt,) = _norm_proj(x, norm(1), p["conv_w_in"], j, 2, [F32], tm=tl["proj_tm"])
            prev = (jnp.zeros((b, CONV_WIDTH - 1, d), F32) if conv_states is None else conv_states[j])
            z, state = _conv_gate(bg, cg, xt, p["conv_w"], prev, j, batch=b, ts=tl["conv_ts"])
            x = _proj_residual(x, z, p["conv_w_out"], j, tm=tl["proj_tm"])
            new_states += [state]
        x = _ffn(x, norm(2), p["ffn_w_gu"], p["ffn_w_down"], i, 1, tm=tl["ffn_tm"], tf=tl["ffn_tf"])
    y = _final_norm(x, p["final_norm_g"].reshape(1, d), tm=tl["proj_tm"])
    return y.reshape(b, seq, d), new_states


def kernel(x_prompt, x_sample, cache_k_l0, cache_v_l0, state_conv_l1, cache_k_l2, cache_v_l2, state_conv_l3,
           norm_g, final_norm_g, ffn_w_gu, ffn_w_down, attn_w_qkv, attn_w_o, attn_lambda_q, attn_lambda_k,
           attn_subln_g, conv_w_in, conv_w, conv_w_out):
    p = dict(
        norm_g=norm_g, final_norm_g=final_norm_g,
        ffn_w_gu=ffn_w_gu.astype(BF16), ffn_w_down=ffn_w_down.astype(BF16),
        attn_w_qkv=attn_w_qkv.astype(BF16), attn_w_o=attn_w_o.astype(BF16),
        attn_lambda_q=attn_lambda_q, attn_lambda_k=attn_lambda_k, attn_subln_g=attn_subln_g,
        conv_w_in=conv_w_in.astype(BF16), conv_w=conv_w, conv_w_out=conv_w_out.astype(BF16),
    )
    y_p, new_p = _trunk(x_prompt, x_prompt.shape[0], None, None, p)
    y_s, new_s = _trunk(x_sample, x_sample.shape[0],
                        ((cache_k_l0, cache_v_l0), (cache_k_l2, cache_v_l2)),
                        (state_conv_l1, state_conv_l3), p)
    return (y_p, y_s, *new_p, *new_s)
```

```python
import functools
import math

import jax
import jax.numpy as jnp
from jax import lax
from jax.experimental import pallas as pl
from jax.experimental.pallas import tpu as pltpu

F32 = jnp.float32
BF16 = jnp.bfloat16

EPS = 1e-5
N_HEADS = 8
HEAD_DIM = 128
HEAD_WIDTH = 2 * HEAD_DIM
QUERY_SCALE = HEAD_DIM ** -0.5 * math.log2(math.e)
CHUNK = 64
CONV_WIDTH = 3
CARRY_ROWS = 8
LANES = 128
ROW_CHUNK = 32

V7X_VMEM_BYTES = 64 * 1024 * 1024
VMEM_LIMIT = V7X_VMEM_BYTES - 4 * 1024 * 1024


def _params(*semantics):
    return pltpu.CompilerParams(dimension_semantics=semantics, vmem_limit_bytes=VMEM_LIMIT)


def _rms_rows(x, g):
    ms = jnp.mean(x * x, axis=-1, keepdims=True)
    return x * lax.rsqrt(ms + EPS) * g


def _ffn_kernel(x_ref, g_ref, wg_ref, wu_ref, wd_ref, *rest, out_norm):
    o_ref, h_ref = rest[-2:]
    f = pl.program_id(1)

    @pl.when(f == 0)
    def _():
        x = x_ref[...]
        h_ref[...] = _rms_rows(x, g_ref[...]).astype(BF16)
        o_ref[...] = x

    h = h_ref[...]
    gate = jnp.dot(h, wg_ref[...], preferred_element_type=F32)
    up = jnp.dot(h, wu_ref[...], preferred_element_type=F32)
    a = (gate * jax.nn.sigmoid(gate) * up * 0.5).astype(BF16)
    o_ref[...] += jnp.dot(a, wd_ref[...], preferred_element_type=F32)

    if out_norm:
        @pl.when(f == pl.num_programs(1) - 1)
        def _():
            o_ref[...] = _rms_rows(o_ref[...], rest[0][...])


def _ffn(x, g, w_gu, w_down, layer, half, *, tm, tf, out_g=None):
    t, d = x.shape
    d_ff = w_down.shape[2]
    nf = d_ff // tf
    assert t % tm == 0 and d_ff % tf == 0
    row = pl.BlockSpec((1, d), lambda i, f: (0, 0))
    extra = [] if out_g is None else [out_g]
    return pl.pallas_call(
        functools.partial(_ffn_kernel, out_norm=out_g is not None),
        out_shape=jax.ShapeDtypeStruct((t, d), F32),
        grid=(t // tm, nf),
        in_specs=[
            pl.BlockSpec((tm, d), lambda i, f: (i, 0)),
            row,
            pl.BlockSpec((None, None, d, tf), lambda i, f: (layer, half, 0, f)),
            pl.BlockSpec((None, None, d, tf), lambda i, f: (layer, half, 0, f + nf)),
            pl.BlockSpec((None, None, tf, d), lambda i, f: (layer, half, f, 0)),
        ] + [row] * len(extra),
        out_specs=pl.BlockSpec((tm, d), lambda i, f: (i, 0)),
        scratch_shapes=[pltpu.VMEM((tm, d), BF16)],
        compiler_params=_params("parallel", "arbitrary"),
        name="ffn",
    )(x, g, w_gu, w_gu, w_down, *extra)


def _qkv_kernel(x_ref, g_ref, w_ref, q_ref, k_ref, kb_ref, v_ref, vb_ref, h_ref, *, nj):
    j = pl.program_id(1)

    @pl.when(j == 0)
    def _():
        h_ref[...] = _rms_rows(x_ref[...], g_ref[...]).astype(BF16)

    y = jnp.dot(h_ref[...], w_ref[...], preferred_element_type=F32)

    @pl.when(j < nj)
    def _():
        q_ref[...] = (y * QUERY_SCALE).astype(BF16)

    @pl.when(jnp.logical_and(j >= nj, j < 2 * nj))
    def _():
        k_ref[...] = y
        kb_ref[...] = y.astype(BF16)

    @pl.when(j >= 2 * nj)
    def _():
        v_ref[...] = y
        vb_ref[...] = y.astype(BF16)


def _qkv(x, g, w, layer, *, tm, tn):
    t, d = x.shape
    assert t % tm == 0 and d % tn == 0
    nj = d // tn
    part = lambda n: pl.BlockSpec((tm, tn), lambda i, j: (i, jnp.clip(j - n * nj, 0, nj - 1)))
    return pl.pallas_call(
        functools.partial(_qkv_kernel, nj=nj),
        out_shape=[jax.ShapeDtypeStruct((t, d), dt) for dt in (BF16, F32, BF16, F32, BF16)],
        grid=(t // tm, 3 * nj),
        in_specs=[
            pl.BlockSpec((tm, d), lambda i, j: (i, 0)),
            pl.BlockSpec((1, d), lambda i, j: (0, 0)),
            pl.BlockSpec((None, d, tn), lambda i, j: (layer, 0, j)),
        ],
        out_specs=[part(0), part(1), part(1), part(2), part(2)],
        scratch_shapes=[pltpu.VMEM((tm, d), BF16)],
        compiler_params=_params("parallel", "arbitrary"),
        name="qkv",
    )(x, g, w)


def _proj_residual_kernel(x_ref, a_ref, w_ref, o_ref):
    o_ref[...] = x_ref[...] + jnp.dot(a_ref[...], w_ref[...], preferred_element_type=F32)


def _proj_residual(x, a, w, layer, *, tm):
    t, d = x.shape
    assert t % tm == 0
    return pl.pallas_call(
        _proj_residual_kernel,
        out_shape=jax.ShapeDtypeStruct((t, d), F32),
        grid=(t // tm,),
        in_specs=[
            pl.BlockSpec((tm, d), lambda i: (i, 0)),
            pl.BlockSpec((tm, d), lambda i: (i, 0)),
            pl.BlockSpec((None, d, d), lambda i: (layer, 0, 0)),
        ],
        out_specs=pl.BlockSpec((tm, d), lambda i: (i, 0)),
        compiler_params=_params("parallel"),
        name="proj_residual",
    )(x, a, w)


def _diff_lambda(lq_ref, lk_ref, lam_init):
    e = jnp.exp(jnp.sum(lq_ref[...] * lk_ref[...], axis=1, keepdims=True))
    return e[0:1] - e[1:2] + lam_init


def _sub_norm(o, sg, lam_init):
    return _rms_rows(o, sg) * (1.0 - lam_init)


def _scores(q, k):
    return lax.dot_general(q, k, (((1,), (1,)), ((), ())), preferred_element_type=F32)


def _attn_prompt_kernel(lq_ref, lk_ref, sg_ref, q_ref, k_ref, v_ref, o_ref,
                        s_ref, pa_ref, pb_ref, aa_ref, ab_ref, m_ref, l_ref, acc_ref, *, lam_init, t):
    qi = pl.program_id(2)
    m_ref[...] = jnp.full(m_ref.shape, -jnp.inf, F32)
    l_ref[...] = jnp.zeros(l_ref.shape, F32)
    acc_ref[...] = jnp.zeros(acc_ref.shape, F32)

    def key_rows(j):
        return pl.ds(pl.multiple_of(j * t, t), t)

    def scores_softmax(j, p_ref, a_ref, diagonal):
        keys = key_rows(j)
        if diagonal:
            r_chunk = lax.broadcasted_iota(jnp.int32, (t, 1), 0) // CHUNK
            k_chunk = lax.broadcasted_iota(jnp.int32, (1, t), 1) // CHUNK
            visible = k_chunk <= r_chunk
        for c in range(2):
            cols = slice(c * HEAD_DIM, (c + 1) * HEAD_DIM)
            s = _scores(q_ref[:, cols], k_ref[keys, cols])
            s_ref[c] = jnp.where(visible, s, -jnp.inf) if diagonal else s
        for c in range(2):
            m_prev = m_ref[c]
            m_new = jnp.maximum(m_prev, jnp.max(s_ref[c], axis=1, keepdims=True))
            alpha = jnp.exp2(m_prev - m_new)
            a_ref[c] = alpha
            m_ref[c] = m_new
            for r in range(0, t, ROW_CHUNK):
                rows = slice(r, r + ROW_CHUNK)
                pr = jnp.exp2(s_ref[c, rows, :] - jnp.tile(m_new[rows], (1, t // LANES)))
                l_ref[c, rows, :] = (alpha[rows] * l_ref[c, rows, :]
                                     + sum(pr[:, i * LANES:(i + 1) * LANES] for i in range(t // LANES)))
                p_ref[c, rows, :] = pr.astype(BF16)

    def value_product(j, p_ref, a_ref):
        v = v_ref[key_rows(j), :]
        for c in range(2):
            acc_ref[c] = (jnp.tile(a_ref[c], (1, HEAD_WIDTH // LANES)) * acc_ref[c]
                          + jnp.dot(p_ref[c], v, preferred_element_type=F32))

    scores_softmax(qi, pa_ref, aa_ref, True)

    def pair(i, carry):
        u = 2 * i
        scores_softmax(u, pb_ref, ab_ref, False)
        value_product(jnp.where(i == 0, qi, u - 1), pa_ref, aa_ref)
        scores_softmax(u + 1, pa_ref, aa_ref, False)
        value_product(u, pb_ref, ab_ref)
        return carry

    lax.fori_loop(0, qi // 2, pair, 0)

    @pl.when(qi % 2 == 1)
    def _():
        scores_softmax(qi - 1, pb_ref, ab_ref, False)
        value_product(jnp.where(qi == 1, qi, qi - 2), pa_ref, aa_ref)
        value_product(qi - 1, pb_ref, ab_ref)

    @pl.when(qi % 2 == 0)
    def _():
        value_product(jnp.where(qi == 0, qi, qi - 1), pa_ref, aa_ref)

    lam = _diff_lambda(lq_ref, lk_ref, lam_init)
    inv = [1.0 / jnp.sum(l_ref[c], axis=1, keepdims=True) for c in range(2)]
    o = acc_ref[0] * inv[0] - lam * (acc_ref[1] * inv[1])
    o_ref[...] = _sub_norm(o, sg_ref[...], lam_init).astype(o_ref.dtype)


def _attn_prompt(q, k, v, lq, lk, sg, lam_init, *, batch, t):
    tokens, d = q.shape
    s = tokens // batch
    assert s % t == 0 and t % CHUNK == 0 and t % LANES == 0 and t % ROW_CHUNK == 0
    nq = s // t
    const = lambda b, h, qi: (0, 0)
    stream_head = lambda b, h, qi: (b, h)
    q_block = lambda b, h, qi: (b * nq + qi, h)
    resident = dict(pipeline_mode=pl.Buffered(1))
    return pl.pallas_call(
        functools.partial(_attn_prompt_kernel, lam_init=lam_init, t=t),
        out_shape=jax.ShapeDtypeStruct((tokens, d), BF16),
        grid=(batch, N_HEADS, nq),
        in_specs=[
            pl.BlockSpec((2, HEAD_DIM), const),
            pl.BlockSpec((2, HEAD_DIM), const),
            pl.BlockSpec((1, HEAD_WIDTH), const),
            pl.BlockSpec((t, HEAD_WIDTH), q_block),
            pl.BlockSpec((s, HEAD_WIDTH), stream_head, **resident),
            pl.BlockSpec((s, HEAD_WIDTH), stream_head, **resident),
        ],
        out_specs=pl.BlockSpec((t, HEAD_WIDTH), q_block),
        scratch_shapes=[
            pltpu.VMEM((2, t, t), F32),
            pltpu.VMEM((2, t, t), BF16),
            pltpu.VMEM((2, t, t), BF16),
            pltpu.VMEM((2, t, LANES), F32),
            pltpu.VMEM((2, t, LANES), F32),
            pltpu.VMEM((2, t, LANES), F32),
            pltpu.VMEM((2, t, LANES), F32),
            pltpu.VMEM((2, t, HEAD_WIDTH), F32),
        ],
        compiler_params=_params("parallel", "parallel", "arbitrary"),
        name="attn_prompt",
    )(lq, lk, sg, q, k, v)


def _attn_sample_kernel(lq_ref, lk_ref, sg_ref, q_ref, kn_ref, vn_ref, kc_ref, vc_ref, o_ref, *, lam_init):
    lam = _diff_lambda(lq_ref, lk_ref, lam_init)
    vc = vc_ref[...].astype(BF16)
    vn = vn_ref[...]
    outs = []
    for c in range(2):
        cols = slice(c * HEAD_DIM, (c + 1) * HEAD_DIM)
        q = q_ref[:, cols]
        s_c = _scores(q, kc_ref[:, cols].astype(BF16))
        s_n = _scores(q, kn_ref[:, cols])
        m = jnp.maximum(jnp.max(s_c, axis=1, keepdims=True), jnp.max(s_n, axis=1, keepdims=True))
        p_c = jnp.exp2(s_c - m)
        p_n = jnp.exp2(s_n - m)
        l = jnp.sum(p_c, axis=1, keepdims=True) + jnp.sum(p_n, axis=1, keepdims=True)
        acc = (jnp.dot(p_c.astype(BF16), vc, preferred_element_type=F32)
               + jnp.dot(p_n.astype(BF16), vn, preferred_element_type=F32))
        outs.append(acc / l)
    o = outs[0] - lam * outs[1]
    o_ref[...] = _sub_norm(o, sg_ref[...], lam_init).astype(o_ref.dtype)


def _attn_sample(q, k_new, v_new, cache_k, cache_v, lq, lk, sg, lam_init, *, batch):
    t, d = q.shape
    ts = t // batch
    past = cache_k.shape[0] // batch
    head = lambda b, h: (b, h)
    return pl.pallas_call(
        functools.partial(_attn_sample_kernel, lam_init=lam_init),
        out_shape=jax.ShapeDtypeStruct((t, d), BF16),
        grid=(batch, N_HEADS),
        in_specs=[
            pl.BlockSpec((2, HEAD_DIM), lambda b, h: (0, 0)),
            pl.BlockSpec((2, HEAD_DIM), lambda b, h: (0, 0)),
            pl.BlockSpec((1, HEAD_WIDTH), lambda b, h: (0, 0)),
            pl.BlockSpec((ts, HEAD_WIDTH), head),
            pl.BlockSpec((ts, HEAD_WIDTH), head),
            pl.BlockSpec((ts, HEAD_WIDTH), head),
            pl.BlockSpec((past, HEAD_WIDTH), head),
            pl.BlockSpec((past, HEAD_WIDTH), head),
        ],
        out_specs=pl.BlockSpec((ts, HEAD_WIDTH), head),
        compiler_params=_params("parallel", "parallel"),
        name="attn_sample",
    )(lq, lk, sg, q, k_new, v_new, cache_k, cache_v)


def _conv_in_kernel(x_ref, g_ref, wb_ref, wc_ref, wx_ref, cw_ref, prev_ref, z_ref, state_ref,
                    h_ref, gbuf_ref, carry_ref, *, ts, tn):
    si = pl.program_id(1)
    j = pl.program_id(2)
    cols = pl.ds(pl.multiple_of(j * tn, tn), tn)
    lo = CARRY_ROWS

    @pl.when(j == 0)
    def _():
        h_ref[...] = _rms_rows(x_ref[...], g_ref[...]).astype(BF16)

    @pl.when(si == 0)
    def _():
        gbuf_ref[lo - (CONV_WIDTH - 1):lo, :] = prev_ref[...]

    @pl.when(si > 0)
    def _():
        gbuf_ref[0:lo, :] = carry_ref[:, cols]

    h = h_ref[...]
    b_gate = jnp.dot(h, wb_ref[...], preferred_element_type=F32)
    c_gate = jnp.dot(h, wc_ref[...], preferred_element_type=F32)
    xt = jnp.dot(h, wx_ref[...], preferred_element_type=F32)
    gbuf_ref[lo:lo + ts, :] = c_gate * xt
    y = (cw_ref[0:1, :] * gbuf_ref[lo - 2:lo - 2 + ts, :]
         + cw_ref[1:2, :] * gbuf_ref[lo - 1:lo - 1 + ts, :]
         + cw_ref[2:3, :] * gbuf_ref[lo:lo + ts, :])
    z_ref[...] = (b_gate * y).astype(z_ref.dtype)
    carry_ref[:, cols] = gbuf_ref[ts:ts + lo, :]
    state_ref[...] = gbuf_ref[lo + ts - (CONV_WIDTH - 1):lo + ts, :]


def _conv_in(x, g, w_in, conv_w, prev, layer, *, batch, ts, tn):
    t, d = x.shape
    s = t // batch
    assert s % ts == 0 and ts >= CARRY_ROWS and d % tn == 0 and tn % LANES == 0
    ns, nj = s // ts, d // tn
    w_block = lambda part: pl.BlockSpec((None, d, tn), lambda bi, si, j: (layer, 0, j + part * nj))
    state = pl.BlockSpec((None, CONV_WIDTH - 1, tn), lambda bi, si, j: (bi, 0, j))
    tile_state = pl.BlockSpec((None, None, CONV_WIDTH - 1, tn), lambda bi, si, j: (bi, si, 0, j))
    z, states = pl.pallas_call(
        functools.partial(_conv_in_kernel, ts=ts, tn=tn),
        out_shape=[jax.ShapeDtypeStruct((t, d), BF16),
                   jax.ShapeDtypeStruct((batch, ns, CONV_WIDTH - 1, d), F32)],
        grid=(batch, ns, nj),
        in_specs=[pl.BlockSpec((ts, d), lambda bi, si, j: (bi * ns + si, 0)),
                  pl.BlockSpec((1, d), lambda bi, si, j: (0, 0)),
                  w_block(0), w_block(1), w_block(2),
                  pl.BlockSpec((None, CONV_WIDTH, tn), lambda bi, si, j: (layer, 0, j)),
                  state],
        out_specs=[pl.BlockSpec((ts, tn), lambda bi, si, j: (bi * ns + si, j)), tile_state],
        scratch_shapes=[pltpu.VMEM((ts, d), BF16),
                        pltpu.VMEM((CARRY_ROWS + ts, tn), F32),
                        pltpu.VMEM((CARRY_ROWS, d), F32)],
        compiler_params=_params("parallel", "arbitrary", "arbitrary"),
        name="conv_in",
    )(x, g, w_in, w_in, w_in, conv_w, prev)
    return z, states[:, ns - 1]


def _tiles(tokens, seq):
    return dict(
        ffn_tm=min(tokens, 1024),
        ffn_tf=512,
        proj_tm=min(tokens, 512),
        qkv_tm=min(tokens, 1024),
        qkv_tn=512,
        attn_t=min(seq, 1024),
        conv_ts=min(seq, 1024),
        conv_tn=512,
    )


def _trunk(x, batch, caches, conv_states, p):
    b, seq, d = x.shape
    t = b * seq
    x = x.reshape(t, d)
    tl = _tiles(t, seq)
    depth = p["norm_g"].shape[0]
    new_states = []
    for i in range(depth):
        j = i // 2
        norm = lambda n: p["norm_g"][i, n].reshape(1, d)
        x = _ffn(x, norm(0), p["ffn_w_gu"], p["ffn_w_down"], i, 0, tm=tl["ffn_tm"], tf=tl["ffn_tf"])
        if i % 2 == 0:
            lam_init = 0.8 - 0.6 * math.exp(-0.3 * i)
            q, k, kb, v, vb = _qkv(x, norm(1), p["attn_w_qkv"], j, tm=tl["qkv_tm"], tn=tl["qkv_tn"])
            lq, lk = p["attn_lambda_q"][j], p["attn_lambda_k"][j]
            sg = p["attn_subln_g"][j].reshape(1, HEAD_WIDTH)
            if caches is None:
                o = _attn_prompt(q, kb, vb, lq, lk, sg, lam_init, batch=b, t=tl["attn_t"])
            else:
                ck, cv = caches[j]
                o = _attn_sample(q, kb, vb, ck.reshape(-1, d), cv.reshape(-1, d), lq, lk, sg, lam_init,
                                 batch=b)
            x = _proj_residual(x, o, p["attn_w_o"], j, tm=tl["proj_tm"])
            new_states += [k.reshape(b, seq, 2 * N_HEADS, HEAD_DIM), v.reshape(b, seq, N_HEADS, HEAD_WIDTH)]
        else:
            prev = (jnp.zeros((b, CONV_WIDTH - 1, d), F32) if conv_states is None else conv_states[j])
            z, state = _conv_in(x, norm(1), p["conv_w_in"], p["conv_w"], prev, j, batch=b,
                                ts=tl["conv_ts"], tn=tl["conv_tn"])
            x = _proj_residual(x, z, p["conv_w_out"], j, tm=tl["proj_tm"])
            new_states += [state]
        out_g = p["final_norm_g"].reshape(1, d) if i == depth - 1 else None
        x = _ffn(x, norm(2), p["ffn_w_gu"], p["ffn_w_down"], i, 1, tm=tl["ffn_tm"], tf=tl["ffn_tf"],
                 out_g=out_g)
    return x.reshape(b, seq, d), new_states


def kernel(x_prompt, x_sample, cache_k_l0, cache_v_l0, state_conv_l1, cache_k_l2, cache_v_l2, state_conv_l3,
           norm_g, final_norm_g, ffn_w_gu, ffn_w_down, attn_w_qkv, attn_w_o, attn_lambda_q, attn_lambda_k,
           attn_subln_g, conv_w_in, conv_w, conv_w_out):
    p = dict(
        norm_g=norm_g, final_norm_g=final_norm_g,
        ffn_w_gu=ffn_w_gu.astype(BF16), ffn_w_down=ffn_w_down.astype(BF16),
        attn_w_qkv=attn_w_qkv.astype(BF16), attn_w_o=attn_w_o.astype(BF16),
        attn_lambda_q=attn_lambda_q, attn_lambda_k=attn_lambda_k, attn_subln_g=attn_subln_g,
        conv_w_in=conv_w_in.astype(BF16), conv_w=conv_w, conv_w_out=conv_w_out.astype(BF16),
    )
    y_p, new_p = _trunk(x_prompt, x_prompt.shape[0], None, None, p)
    y_s, new_s = _trunk(x_sample, x_sample.shape[0],
                        ((cache_k_l0, cache_v_l0), (cache_k_l2, cache_v_l2)),
                        (state_conv_l1, state_conv_l3), p)
    return (y_p, y_s, *new_p, *new_s)
```

```python
import functools
import math

import jax
import jax.numpy as jnp
from jax import lax
from jax.experimental import pallas as pl
from jax.experimental.pallas import tpu as pltpu

F32 = jnp.float32
BF16 = jnp.bfloat16

EPS = 1e-5
N_HEADS = 8
HEAD_DIM = 128
HEAD_WIDTH = 2 * HEAD_DIM
QUERY_SCALE = HEAD_DIM ** -0.5 * math.log2(math.e)
CHUNK = 64
CONV_WIDTH = 3
CARRY_ROWS = 8
LANES = 128
ROW_CHUNK = 32

V7X_VMEM_BYTES = 64 * 1024 * 1024
VMEM_LIMIT = V7X_VMEM_BYTES - 4 * 1024 * 1024


def _params(*semantics):
    return pltpu.CompilerParams(dimension_semantics=semantics, vmem_limit_bytes=VMEM_LIMIT)


def _rms_rows(x, g):
    ms = jnp.mean(x * x, axis=-1, keepdims=True)
    return x * lax.rsqrt(ms + EPS) * g


def _ffn_kernel(x_ref, g_ref, wg_ref, wu_ref, wd_ref, *rest, out_norm):
    o_ref, h_ref = rest[-2:]
    f = pl.program_id(1)

    @pl.when(f == 0)
    def _():
        x = x_ref[...]
        h_ref[...] = _rms_rows(x, g_ref[...]).astype(BF16)
        o_ref[...] = x

    h = h_ref[...]
    gate = jnp.dot(h, wg_ref[...], preferred_element_type=F32)
    up = jnp.dot(h, wu_ref[...], preferred_element_type=F32)
    a = (gate * jax.nn.sigmoid(gate) * up * 0.5).astype(BF16)
    o_ref[...] += jnp.dot(a, wd_ref[...], preferred_element_type=F32)

    if out_norm:
        @pl.when(f == pl.num_programs(1) - 1)
        def _():
            o_ref[...] = _rms_rows(o_ref[...], rest[0][...])


def _ffn(x, g, w_gu, w_down, layer, half, *, tm, tf, out_g=None):
    t, d = x.shape
    d_ff = w_down.shape[2]
    nf = d_ff // tf
    assert t % tm == 0 and d_ff % tf == 0
    row = pl.BlockSpec((1, d), lambda i, f: (0, 0))
    extra = [] if out_g is None else [out_g]
    return pl.pallas_call(
        functools.partial(_ffn_kernel, out_norm=out_g is not None),
        out_shape=jax.ShapeDtypeStruct((t, d), F32),
        grid=(t // tm, nf),
        in_specs=[
            pl.BlockSpec((tm, d), lambda i, f: (i, 0)),
            row,
            pl.BlockSpec((None, None, d, tf), lambda i, f: (layer, half, 0, f)),
            pl.BlockSpec((None, None, d, tf), lambda i, f: (layer, half, 0, f + nf)),
            pl.BlockSpec((None, None, tf, d), lambda i, f: (layer, half, f, 0)),
        ] + [row] * len(extra),
        out_specs=pl.BlockSpec((tm, d), lambda i, f: (i, 0)),
        scratch_shapes=[pltpu.VMEM((tm, d), BF16)],
        compiler_params=_params("parallel", "arbitrary"),
        name="ffn",
    )(x, g, w_gu, w_gu, w_down, *extra)


def _norm_proj_kernel(x_ref, g_ref, w_ref, *o_refs, scale):
    h = _rms_rows(x_ref[...], g_ref[...]).astype(BF16)
    y = jnp.dot(h, w_ref[...], preferred_element_type=F32)
    if scale != 1.0:
        y = y * scale
    for o_ref in o_refs:
        o_ref[...] = y.astype(o_ref.dtype)


def _norm_proj(x, g, w, layer, col, out_dtypes, *, tm, scale=1.0):
    t, d = x.shape
    assert t % tm == 0
    outs = pl.pallas_call(
        functools.partial(_norm_proj_kernel, scale=scale),
        out_shape=[jax.ShapeDtypeStruct((t, d), dt) for dt in out_dtypes],
        grid=(t // tm,),
        in_specs=[
            pl.BlockSpec((tm, d), lambda i: (i, 0)),
            pl.BlockSpec((1, d), lambda i: (0, 0)),
            pl.BlockSpec((None, d, d), lambda i: (layer, 0, col)),
        ],
        out_specs=[pl.BlockSpec((tm, d), lambda i: (i, 0)) for _ in out_dtypes],
        compiler_params=_params("parallel"),
        name="norm_proj",
    )(x, g, w)
    return outs


def _proj_residual_kernel(x_ref, a_ref, w_ref, o_ref):
    o_ref[...] = x_ref[...] + jnp.dot(a_ref[...], w_ref[...], preferred_element_type=F32)


def _proj_residual(x, a, w, layer, *, tm):
    t, d = x.shape
    assert t % tm == 0
    return pl.pallas_call(
        _proj_residual_kernel,
        out_shape=jax.ShapeDtypeStruct((t, d), F32),
        grid=(t // tm,),
        in_specs=[
            pl.BlockSpec((tm, d), lambda i: (i, 0)),
            pl.BlockSpec((tm, d), lambda i: (i, 0)),
            pl.BlockSpec((None, d, d), lambda i: (layer, 0, 0)),
        ],
        out_specs=pl.BlockSpec((tm, d), lambda i: (i, 0)),
        compiler_params=_params("parallel"),
        name="proj_residual",
    )(x, a, w)


def _diff_lambda(lq_ref, lk_ref, lam_init):
    e = jnp.exp(jnp.sum(lq_ref[...] * lk_ref[...], axis=1, keepdims=True))
    return e[0:1] - e[1:2] + lam_init


def _sub_norm(o, sg, lam_init):
    return _rms_rows(o, sg) * (1.0 - lam_init)


def _scores(q, k):
    return lax.dot_general(q, k, (((1,), (1,)), ((), ())), preferred_element_type=F32)


def _attn_prompt_kernel(lq_ref, lk_ref, sg_ref, q_ref, k_ref, v_ref, o_ref,
                        s_ref, pa_ref, pb_ref, aa_ref, ab_ref, m_ref, l_ref, acc_ref, *, lam_init, t):
    qi = pl.program_id(2)
    m_ref[...] = jnp.full(m_ref.shape, -jnp.inf, F32)
    l_ref[...] = jnp.zeros(l_ref.shape, F32)
    acc_ref[...] = jnp.zeros(acc_ref.shape, F32)

    def key_rows(j):
        return pl.ds(pl.multiple_of(j * t, t), t)

    def scores_softmax(j, p_ref, a_ref, diagonal):
        keys = key_rows(j)
        if diagonal:
            r_chunk = lax.broadcasted_iota(jnp.int32, (t, 1), 0) // CHUNK
            k_chunk = lax.broadcasted_iota(jnp.int32, (1, t), 1) // CHUNK
            visible = k_chunk <= r_chunk
        for c in range(2):
            cols = slice(c * HEAD_DIM, (c + 1) * HEAD_DIM)
            s = _scores(q_ref[:, cols], k_ref[keys, cols])
            s_ref[c] = jnp.where(visible, s, -jnp.inf) if diagonal else s
        for c in range(2):
            m_prev = m_ref[c]
            m_new = jnp.maximum(m_prev, jnp.max(s_ref[c], axis=1, keepdims=True))
            alpha = jnp.exp2(m_prev - m_new)
            a_ref[c] = alpha
            m_ref[c] = m_new
            for r in range(0, t, ROW_CHUNK):
                rows = slice(r, r + ROW_CHUNK)
                pr = jnp.exp2(s_ref[c, rows, :] - jnp.tile(m_new[rows], (1, t // LANES)))
                l_ref[c, rows, :] = (alpha[rows] * l_ref[c, rows, :]
                                     + sum(pr[:, i * LANES:(i + 1) * LANES] for i in range(t // LANES)))
                p_ref[c, rows, :] = pr.astype(BF16)

    def value_product(j, p_ref, a_ref):
        v = v_ref[key_rows(j), :]
        for c in range(2):
            acc_ref[c] = (jnp.tile(a_ref[c], (1, HEAD_WIDTH // LANES)) * acc_ref[c]
                          + jnp.dot(p_ref[c], v, preferred_element_type=F32))

    scores_softmax(qi, pa_ref, aa_ref, True)

    def pair(i, carry):
        u = 2 * i
        scores_softmax(u, pb_ref, ab_ref, False)
        value_product(jnp.where(i == 0, qi, u - 1), pa_ref, aa_ref)
        scores_softmax(u + 1, pa_ref, aa_ref, False)
        value_product(u, pb_ref, ab_ref)
        return carry

    lax.fori_loop(0, qi // 2, pair, 0)

    @pl.when(qi % 2 == 1)
    def _():
        scores_softmax(qi - 1, pb_ref, ab_ref, False)
        value_product(jnp.where(qi == 1, qi, qi - 2), pa_ref, aa_ref)
        value_product(qi - 1, pb_ref, ab_ref)

    @pl.when(qi % 2 == 0)
    def _():
        value_product(jnp.where(qi == 0, qi, qi - 1), pa_ref, aa_ref)

    lam = _diff_lambda(lq_ref, lk_ref, lam_init)
    inv = [1.0 / jnp.sum(l_ref[c], axis=1, keepdims=True) for c in range(2)]
    o = acc_ref[0] * inv[0] - lam * (acc_ref[1] * inv[1])
    o_ref[...] = _sub_norm(o, sg_ref[...], lam_init).astype(o_ref.dtype)


def _attn_prompt(q, k, v, lq, lk, sg, lam_init, *, batch, t):
    tokens, d = q.shape
    s = tokens // batch
    assert s % t == 0 and t % CHUNK == 0 and t % LANES == 0 and t % ROW_CHUNK == 0
    nq = s // t
    const = lambda b, h, qi: (0, 0)
    stream_head = lambda b, h, qi: (b, h)
    q_block = lambda b, h, qi: (b * nq + qi, h)
    resident = dict(pipeline_mode=pl.Buffered(1))
    return pl.pallas_call(
        functools.partial(_attn_prompt_kernel, lam_init=lam_init, t=t),
        out_shape=jax.ShapeDtypeStruct((tokens, d), BF16),
        grid=(batch, N_HEADS, nq),
        in_specs=[
            pl.BlockSpec((2, HEAD_DIM), const),
            pl.BlockSpec((2, HEAD_DIM), const),
            pl.BlockSpec((1, HEAD_WIDTH), const),
            pl.BlockSpec((t, HEAD_WIDTH), q_block),
            pl.BlockSpec((s, HEAD_WIDTH), stream_head, **resident),
            pl.BlockSpec((s, HEAD_WIDTH), stream_head, **resident),
        ],
        out_specs=pl.BlockSpec((t, HEAD_WIDTH), q_block),
        scratch_shapes=[
            pltpu.VMEM((2, t, t), F32),
            pltpu.VMEM((2, t, t), BF16),
            pltpu.VMEM((2, t, t), BF16),
            pltpu.VMEM((2, t, LANES), F32),
            pltpu.VMEM((2, t, LANES), F32),
            pltpu.VMEM((2, t, LANES), F32),
            pltpu.VMEM((2, t, LANES), F32),
            pltpu.VMEM((2, t, HEAD_WIDTH), F32),
        ],
        compiler_params=_params("parallel", "parallel", "arbitrary"),
        name="attn_prompt",
    )(lq, lk, sg, q, k, v)


def _attn_sample_kernel(lq_ref, lk_ref, sg_ref, q_ref, kn_ref, vn_ref, kc_ref, vc_ref, o_ref, *, lam_init):
    lam = _diff_lambda(lq_ref, lk_ref, lam_init)
    vc = vc_ref[...].astype(BF16)
    vn = vn_ref[...]
    outs = []
    for c in range(2):
        cols = slice(c * HEAD_DIM, (c + 1) * HEAD_DIM)
        q = q_ref[:, cols]
        s_c = _scores(q, kc_ref[:, cols].astype(BF16))
        s_n = _scores(q, kn_ref[:, cols])
        m = jnp.maximum(jnp.max(s_c, axis=1, keepdims=True), jnp.max(s_n, axis=1, keepdims=True))
        p_c = jnp.exp2(s_c - m)
        p_n = jnp.exp2(s_n - m)
        l = jnp.sum(p_c, axis=1, keepdims=True) + jnp.sum(p_n, axis=1, keepdims=True)
        acc = (jnp.dot(p_c.astype(BF16), vc, preferred_element_type=F32)
               + jnp.dot(p_n.astype(BF16), vn, preferred_element_type=F32))
        outs.append(acc / l)
    o = outs[0] - lam * outs[1]
    o_ref[...] = _sub_norm(o, sg_ref[...], lam_init).astype(o_ref.dtype)


def _attn_sample(q, k_new, v_new, cache_k, cache_v, lq, lk, sg, lam_init, *, batch):
    t, d = q.shape
    ts = t // batch
    past = cache_k.shape[0] // batch
    head = lambda b, h: (b, h)
    return pl.pallas_call(
        functools.partial(_attn_sample_kernel, lam_init=lam_init),
        out_shape=jax.ShapeDtypeStruct((t, d), BF16),
        grid=(batch, N_HEADS),
        in_specs=[
            pl.BlockSpec((2, HEAD_DIM), lambda b, h: (0, 0)),
            pl.BlockSpec((2, HEAD_DIM), lambda b, h: (0, 0)),
            pl.BlockSpec((1, HEAD_WIDTH), lambda b, h: (0, 0)),
            pl.BlockSpec((ts, HEAD_WIDTH), head),
            pl.BlockSpec((ts, HEAD_WIDTH), head),
            pl.BlockSpec((ts, HEAD_WIDTH), head),
            pl.BlockSpec((past, HEAD_WIDTH), head),
            pl.BlockSpec((past, HEAD_WIDTH), head),
        ],
        out_specs=pl.BlockSpec((ts, HEAD_WIDTH), head),
        compiler_params=_params("parallel", "parallel"),
        name="attn_sample",
    )(lq, lk, sg, q, k_new, v_new, cache_k, cache_v)


def _conv_in_kernel(x_ref, g_ref, wb_ref, wc_ref, wx_ref, cw_ref, prev_ref, z_ref, state_ref,
                    h_ref, gbuf_ref, carry_ref, *, ts, tn):
    si = pl.program_id(1)
    j = pl.program_id(2)
    cols = pl.ds(pl.multiple_of(j * tn, tn), tn)
    lo = CARRY_ROWS

    @pl.when(j == 0)
    def _():
        h_ref[...] = _rms_rows(x_ref[...], g_ref[...]).astype(BF16)

    @pl.when(si == 0)
    def _():
        gbuf_ref[lo - (CONV_WIDTH - 1):lo, :] = prev_ref[...]

    @pl.when(si > 0)
    def _():
        gbuf_ref[0:lo, :] = carry_ref[:, cols]

    h = h_ref[...]
    b_gate = jnp.dot(h, wb_ref[...], preferred_element_type=F32)
    c_gate = jnp.dot(h, wc_ref[...], preferred_element_type=F32)
    xt = jnp.dot(h, wx_ref[...], preferred_element_type=F32)
    gbuf_ref[lo:lo + ts, :] = c_gate * xt
    y = (cw_ref[0:1, :] * gbuf_ref[lo - 2:lo - 2 + ts, :]
         + cw_ref[1:2, :] * gbuf_ref[lo - 1:lo - 1 + ts, :]
         + cw_ref[2:3, :] * gbuf_ref[lo:lo + ts, :])
    z_ref[...] = (b_gate * y).astype(z_ref.dtype)
    carry_ref[:, cols] = gbuf_ref[ts:ts + lo, :]
    state_ref[...] = gbuf_ref[lo + ts - (CONV_WIDTH - 1):lo + ts, :]


def _conv_in(x, g, w_in, conv_w, prev, layer, *, batch, ts, tn):
    t, d = x.shape
    s = t // batch
    assert s % ts == 0 and ts >= CARRY_ROWS and d % tn == 0 and tn % LANES == 0
    ns, nj = s // ts, d // tn
    w_block = lambda part: pl.BlockSpec((None, d, tn), lambda bi, si, j: (layer, 0, j + part * nj))
    state = pl.BlockSpec((None, CONV_WIDTH - 1, tn), lambda bi, si, j: (bi, 0, j))
    tile_state = pl.BlockSpec((None, None, CONV_WIDTH - 1, tn), lambda bi, si, j: (bi, si, 0, j))
    z, states = pl.pallas_call(
        functools.partial(_conv_in_kernel, ts=ts, tn=tn),
        out_shape=[jax.ShapeDtypeStruct((t, d), BF16),
                   jax.ShapeDtypeStruct((batch, ns, CONV_WIDTH - 1, d), F32)],
        grid=(batch, ns, nj),
        in_specs=[pl.BlockSpec((ts, d), lambda bi, si, j: (bi * ns + si, 0)),
                  pl.BlockSpec((1, d), lambda bi, si, j: (0, 0)),
                  w_block(0), w_block(1), w_block(2),
                  pl.BlockSpec((None, CONV_WIDTH, tn), lambda bi, si, j: (layer, 0, j)),
                  state],
        out_specs=[pl.BlockSpec((ts, tn), lambda bi, si, j: (bi * ns + si, j)), tile_state],
        scratch_shapes=[pltpu.VMEM((ts, d), BF16),
                        pltpu.VMEM((CARRY_ROWS + ts, tn), F32),
                        pltpu.VMEM((CARRY_ROWS, d), F32)],
        compiler_params=_params("parallel", "arbitrary", "arbitrary"),
        name="conv_in",
    )(x, g, w_in, w_in, w_in, conv_w, prev)
    return z, states[:, ns - 1]


def _tiles(tokens, seq):
    return dict(
        ffn_tm=min(tokens, 1024),
        ffn_tf=512,
        proj_tm=min(tokens, 512),
        attn_t=min(seq, 1024),
        conv_ts=min(seq, 1024),
        conv_tn=512,
    )


def _trunk(x, batch, caches, conv_states, p):
    b, seq, d = x.shape
    t = b * seq
    x = x.reshape(t, d)
    tl = _tiles(t, seq)
    depth = p["norm_g"].shape[0]
    new_states = []
    for i in range(depth):
        j = i // 2
        norm = lambda n: p["norm_g"][i, n].reshape(1, d)
        x = _ffn(x, norm(0), p["ffn_w_gu"], p["ffn_w_down"], i, 0, tm=tl["ffn_tm"], tf=tl["ffn_tf"])
        if i % 2 == 0:
            lam_init = 0.8 - 0.6 * math.exp(-0.3 * i)
            (q,) = _norm_proj(x, norm(1), p["attn_w_qkv"], j, 0, [BF16], tm=tl["proj_tm"],
                              scale=QUERY_SCALE)
            k, kb = _norm_proj(x, norm(1), p["attn_w_qkv"], j, 1, [F32, BF16], tm=tl["proj_tm"])
            v, vb = _norm_proj(x, norm(1), p["attn_w_qkv"], j, 2, [F32, BF16], tm=tl["proj_tm"])
            lq, lk = p["attn_lambda_q"][j], p["attn_lambda_k"][j]
            sg = p["attn_subln_g"][j].reshape(1, HEAD_WIDTH)
            if caches is None:
                o = _attn_prompt(q, kb, vb, lq, lk, sg, lam_init, batch=b, t=tl["attn_t"])
            else:
                ck, cv = caches[j]
                o = _attn_sample(q, kb, vb, ck.reshape(-1, d), cv.reshape(-1, d), lq, lk, sg, lam_init,
                                 batch=b)
            x = _proj_residual(x, o, p["attn_w_o"], j, tm=tl["proj_tm"])
            new_states += [k.reshape(b, seq, 2 * N_HEADS, HEAD_DIM), v.reshape(b, seq, N_HEADS, HEAD_WIDTH)]
        else:
            prev = (jnp.zeros((b, CONV_WIDTH - 1, d), F32) if conv_states is None else conv_states[j])
            z, state = _conv_in(x, norm(1), p["conv_w_in"], p["conv_w"], prev, j, batch=b,
                                ts=tl["conv_ts"], tn=tl["conv_tn"])
            x = _proj_residual(x, z, p["conv_w_out"], j, tm=tl["proj_tm"])
            new_states += [state]
        out_g = p["final_norm_g"].reshape(1, d) if i == depth - 1 else None
        x = _ffn(x, norm(2), p["ffn_w_gu"], p["ffn_w_down"], i, 1, tm=tl["ffn_tm"], tf=tl["ffn_tf"],
                 out_g=out_g)
    return x.reshape(b, seq, d), new_states


def kernel(x_prompt, x_sample, cache_k_l0, cache_v_l0, state_conv_l1, cache_k_l2, cache_v_l2, state_conv_l3,
           norm_g, final_norm_g, ffn_w_gu, ffn_w_down, attn_w_qkv, attn_w_o, attn_lambda_q, attn_lambda_k,
           attn_subln_g, conv_w_in, conv_w, conv_w_out):
    p = dict(
        norm_g=norm_g, final_norm_g=final_norm_g,
        ffn_w_gu=ffn_w_gu.astype(BF16), ffn_w_down=ffn_w_down.astype(BF16),
        attn_w_qkv=attn_w_qkv.astype(BF16), attn_w_o=attn_w_o.astype(BF16),
        attn_lambda_q=attn_lambda_q, attn_lambda_k=attn_lambda_k, attn_subln_g=attn_subln_g,
        conv_w_in=conv_w_in.astype(BF16), conv_w=conv_w, conv_w_out=conv_w_out.astype(BF16),
    )
    y_p, new_p = _trunk(x_prompt, x_prompt.shape[0], None, None, p)
    y_s, new_s = _trunk(x_sample, x_sample.shape[0],
                        ((cache_k_l0, cache_v_l0), (cache_k_l2, cache_v_l2)),
                        (state_conv_l1, state_conv_l3), p)
    return (y_p, y_s, *new_p, *new_s)
```
